```python
import jax, jax.numpy as jnp
from jax import lax
import numpy as np

D_MODEL = 1024
BATCH = 16
SEQ = 2048
DEPTH = 1

GRID_W = 64
Q_BLOCK = 128
EPS = 1e-6
ROPE_THETA = 10000.0

MLA_HEADS = 4
MLA_Q_RANK = 384
MLA_KV_RANK = 256
MLA_NOPE = 128
MLA_ROPE = 64
MLA_V = 128
MLA_WIDTH = MLA_HEADS * MLA_V

GQA_HEADS = 4
GQA_KV_HEADS = 2
GQA_HEAD_DIM = 128
GQA_WIDTH = GQA_HEADS * GQA_HEAD_DIM

MIX_WIDTH = MLA_WIDTH + GQA_WIDTH

OFF_MLA_Q = 0
OFF_MLA_KV = OFF_MLA_Q + MLA_Q_RANK
OFF_MLA_KPE = OFF_MLA_KV + MLA_KV_RANK
OFF_GQA_Q = OFF_MLA_KPE + MLA_ROPE
OFF_GQA_K = OFF_GQA_Q + GQA_HEADS * GQA_HEAD_DIM
OFF_GQA_V = OFF_GQA_K + GQA_KV_HEADS * GQA_HEAD_DIM
IN_WIDTH = OFF_GQA_V + GQA_KV_HEADS * GQA_HEAD_DIM

D_FF = 2816
CONV_W = 3

kernel_name = "hybrid_mla_axialgqa_convffn_block"


def rmsnorm(x, g):
    xf = x.astype(jnp.float32)
    y = xf * lax.rsqrt(jnp.mean(xf * xf, axis=-1, keepdims=True) + EPS)
    return (y * g.astype(jnp.float32)).astype(x.dtype)


def rope_tables(pos, dim):
    inv = ROPE_THETA ** (-jnp.arange(0, dim, 2, dtype=jnp.float32) / dim)
    ang = pos.astype(jnp.float32)[:, None] * inv[None, :]
    return jnp.cos(ang), jnp.sin(ang)


def apply_rope(x, cos, sin):
    half = x.shape[-1] // 2
    x1, x2 = x[..., :half], x[..., half:]
    c = cos[None, :, None, :].astype(x.dtype)
    s = sin[None, :, None, :].astype(x.dtype)
    return jnp.concatenate([x1 * c - x2 * s, x1 * s + x2 * c], axis=-1)


def blocked_attention(q, k, v):
    B, S, H, dq = q.shape
    Hkv, dv = k.shape[2], v.shape[-1]
    G = H // Hkv
    nb = S // Q_BLOCK
    scale = dq ** -0.5
    qb = q.reshape(B, nb, Q_BLOCK, Hkv, G, dq).transpose(1, 0, 2, 3, 4, 5)

    def one_block(q_blk):
        s = jnp.einsum('bqhgd,bkhd->bhgqk', q_blk, k,
                       preferred_element_type=jnp.float32) * scale
        p = jax.nn.softmax(s, axis=-1).astype(v.dtype)
        return jnp.einsum('bhgqk,bkhd->bqhgd', p, v)

    out = lax.map(one_block, qb)
    return out.transpose(1, 0, 2, 3, 4, 5).reshape(B, S, H, dv)


def token_mixer(u, w_in, mla_q_norm_g, mla_w_q_up, mla_kv_norm_g, mla_w_kv_up,
                gqa_q_norm_g, gqa_k_norm_g, group_norm_mla_g, group_norm_gqa_g,
                w_out, cos1, sin1, cos_r, sin_r, cos_c, sin_c):
    B, S, _ = u.shape
    z = u @ w_in

    c_q = rmsnorm(z[..., OFF_MLA_Q:OFF_MLA_KV], mla_q_norm_g)
    c_kv = rmsnorm(z[..., OFF_MLA_KV:OFF_MLA_KPE], mla_kv_norm_g)
    k_pe = z[..., OFF_MLA_KPE:OFF_GQA_Q][:, :, None, :]
    q = (c_q @ mla_w_q_up).reshape(B, S, MLA_HEADS, MLA_NOPE + MLA_ROPE)
    q_pe = apply_rope(q[..., MLA_NOPE:], cos1, sin1)
    q_m = jnp.concatenate([q[..., :MLA_NOPE], q_pe], axis=-1)
    kv = (c_kv @ mla_w_kv_up).reshape(B, S, MLA_HEADS, MLA_NOPE + MLA_V)
    k_pe = jnp.broadcast_to(apply_rope(k_pe, cos1, sin1), (B, S, MLA_HEADS, MLA_ROPE))
    k_m = jnp.concatenate([kv[..., :MLA_NOPE], k_pe], axis=-1)
    o_mla = blocked_attention(q_m, k_m, kv[..., MLA_NOPE:])

    half = GQA_HEAD_DIM // 2
    qg = z[..., OFF_GQA_Q:OFF_GQA_K].reshape(B, S, GQA_HEADS, GQA_HEAD_DIM)
    kg = z[..., OFF_GQA_K:OFF_GQA_V].reshape(B, S, GQA_KV_HEADS, GQA_HEAD_DIM)
    vg = z[..., OFF_GQA_V:IN_WIDTH].reshape(B, S, GQA_KV_HEADS, GQA_HEAD_DIM)
    qg = rmsnorm(qg, gqa_q_norm_g)
    kg = rmsnorm(kg, gqa_k_norm_g)

    def axial(t):
        return jnp.concatenate([apply_rope(t[..., :half], cos_r, sin_r),
                                apply_rope(t[..., half:], cos_c, sin_c)], axis=-1)

    o_gqa = blocked_attention(axial(qg), axial(kg), vg)

    o = jnp.concatenate([rmsnorm(o_mla.reshape(B, S, MLA_WIDTH), group_norm_mla_g),
                         rmsnorm(o_gqa.reshape(B, S, GQA_WIDTH), group_norm_gqa_g)], axis=-1)
    return o @ w_out


def conv_ffn(u, w_up, conv_w, conv_b, w_down):
    hu = u @ w_up
    C = hu.shape[-1]
    hc = lax.conv_general_dilated(
        hu, conv_w[:, None, :].astype(hu.dtype), window_strides=(1,),
        padding=((CONV_W // 2, CONV_W // 2),),
        dimension_numbers=('NWC', 'WIO', 'NWC'), feature_group_count=C) + conv_b
    gate, val = hc[..., :D_FF], hc[..., D_FF:]
    return (jax.nn.silu(gate) * val) @ w_down


def setup_inputs(seed: int = 0) -> dict:
    key = jax.random.key(seed)
    ks = jax.random.split(key, 20)
    f32 = jnp.float32

    def w(k, shape, fan_in):
        return jax.random.normal(k, shape, f32) * fan_in ** -0.5

    def gain(k, shape):
        return 1.0 + 0.02 * jax.random.normal(k, shape, f32)

    L = DEPTH
    return {
        "x": jax.random.normal(ks[0], (BATCH, SEQ, D_MODEL), f32),
        "norm1_g": gain(ks[1], (L, D_MODEL)),
        "w_in": w(ks[2], (L, D_MODEL, IN_WIDTH), D_MODEL),
        "mla_q_norm_g": gain(ks[3], (L, MLA_Q_RANK)),
        "mla_w_q_up": w(ks[4], (L, MLA_Q_RANK, MLA_HEADS * (MLA_NOPE + MLA_ROPE)), MLA_Q_RANK),
        "mla_kv_norm_g": gain(ks[5], (L, MLA_KV_RANK)),
        "mla_w_kv_up": w(ks[6], (L, MLA_KV_RANK, MLA_HEADS * (MLA_NOPE + MLA_V)), MLA_KV_RANK),
        "gqa_q_norm_g": gain(ks[7], (L, GQA_HEAD_DIM)),
        "gqa_k_norm_g": gain(ks[8], (L, GQA_HEAD_DIM)),
        "group_norm_mla_g": gain(ks[9], (L, MLA_WIDTH)),
        "group_norm_gqa_g": gain(ks[10], (L, GQA_WIDTH)),
        "w_out": w(ks[11], (L, MIX_WIDTH, D_MODEL), MIX_WIDTH),
        "norm2_g": gain(ks[12], (L, D_MODEL)),
        "ffn_w_up": w(ks[13], (L, D_MODEL, 2 * D_FF), D_MODEL),
        "ffn_conv_w": w(ks[14], (L, CONV_W, 2 * D_FF), CONV_W),
        "ffn_conv_b": 0.02 * jax.random.normal(ks[15], (L, 2 * D_FF), f32),
        "ffn_w_down": w(ks[16], (L, D_FF, D_MODEL), D_FF),
        "final_norm_g": gain(ks[17], (D_MODEL,)),
    }


def reference(x, norm1_g, w_in, mla_q_norm_g, mla_w_q_up, mla_kv_norm_g, mla_w_kv_up,
              gqa_q_norm_g, gqa_k_norm_g, group_norm_mla_g, group_norm_gqa_g, w_out,
              norm2_g, ffn_w_up, ffn_conv_w, ffn_conv_b, ffn_w_down, final_norm_g):
    S = x.shape[1]
    rows = S // GRID_W
    t = jnp.arange(S, dtype=jnp.int32)
    row = jnp.repeat(jnp.arange(rows, dtype=jnp.int32), GRID_W)
    col = jnp.tile(jnp.arange(GRID_W, dtype=jnp.int32), rows)
    cos1, sin1 = rope_tables(t, MLA_ROPE)
    cos_r, sin_r = rope_tables(row, GQA_HEAD_DIM // 2)
    cos_c, sin_c = rope_tables(col, GQA_HEAD_DIM // 2)

    h = x
    for l in range(DEPTH):
        u = rmsnorm(h, norm1_g[l])
        h = h + token_mixer(u, w_in[l], mla_q_norm_g[l], mla_w_q_up[l], mla_kv_norm_g[l],
                            mla_w_kv_up[l], gqa_q_norm_g[l], gqa_k_norm_g[l],
                            group_norm_mla_g[l], group_norm_gqa_g[l], w_out[l],
                            cos1, sin1, cos_r, sin_r, cos_c, sin_c)
        u = rmsnorm(h, norm2_g[l])
        h = h + conv_ffn(u, ffn_w_up[l], ffn_conv_w[l], ffn_conv_b[l], ffn_w_down[l])
    return rmsnorm(h, final_norm_g)
```

```python
import functools

import jax
import jax.numpy as jnp
from jax import lax
from jax.experimental import pallas as pl
from jax.experimental.pallas import tpu as pltpu

EPS = 1e-6
ROPE_THETA = 10000.0
GRID_W = 64

MLA_HEADS = 4
MLA_Q_RANK = 384
MLA_KV_RANK = 256
MLA_NOPE = 128
MLA_ROPE = 64
MLA_V = 128
MLA_QK = MLA_NOPE + MLA_ROPE

GQA_HEADS = 4
GQA_KV_HEADS = 2
GQA_HEAD_DIM = 128

D_FF = 2816
LANES = 128

Z_CQ = 0
Z_CKV = Z_CQ + MLA_Q_RANK
Z_KPE = Z_CKV + MLA_KV_RANK
Z_GQ = Z_KPE + 2 * MLA_ROPE
Z_GK = Z_GQ + GQA_HEADS * GQA_HEAD_DIM
Z_GV = Z_GK + GQA_KV_HEADS * GQA_HEAD_DIM
Z_WIDTH = Z_GV + GQA_KV_HEADS * GQA_HEAD_DIM

FF_CHUNK = 256
N_FF_CHUNKS = D_FF // FF_CHUNK
HALO = 16

VMEM_LIMIT = 56 * 1024 * 1024


def _rms(x, g):
    y = x * lax.rsqrt(jnp.mean(x * x, axis=-1, keepdims=True) + EPS)
    return y * g


def _rope(x, c, sa, sb):
    return x * c + pltpu.roll(x, 96, 1) * sa + pltpu.roll(x, 32, 1) * sb


def _dot(a, b):
    return jnp.dot(a, b, preferred_element_type=jnp.float32)


def _proj_kernel(x_ref, g1_ref, win_ref, gq_ref, wq_ref, gkv_ref, wkv_ref, ggq_ref, ggk_ref,
                 c1_ref, sa1_ref, sb1_ref, cg_ref, sag_ref, sbg_ref,
                 qm_ref, km_ref, vm_ref, qg_ref, kg_ref, vg_ref):
    bf = jnp.bfloat16
    u = _rms(x_ref[...], g1_ref[...]).astype(bf)
    z = _dot(u, win_ref[...])

    c_q = _rms(z[:, Z_CQ:Z_CKV], gq_ref[...]).astype(bf)
    q = _dot(c_q, wq_ref[...])
    c_kv = _rms(z[:, Z_CKV:Z_KPE], gkv_ref[...]).astype(bf)
    kv = _dot(c_kv, wkv_ref[...])

    c1, sa1, sb1 = c1_ref[...], sa1_ref[...], sb1_ref[...]
    kpe = _rope(z[:, Z_KPE:Z_GQ], c1, sa1, sb1).astype(bf)
    scale_m = MLA_QK ** -0.5
    nope_w = MLA_HEADS * MLA_NOPE
    q_rope = [_rope(q[:, nope_w + LANES * i:nope_w + LANES * (i + 1)], c1, sa1, sb1) * scale_m
              for i in range(MLA_HEADS // 2)]
    lane = lax.broadcasted_iota(jnp.int32, (1, LANES), 1)
    for h in range(MLA_HEADS):
        lo = 2 * LANES * h
        qm_ref[:, lo:lo + LANES] = (q[:, LANES * h:LANES * (h + 1)] * scale_m).astype(bf)
        own = (lane < MLA_ROPE) if h % 2 == 0 else (lane >= MLA_ROPE)
        qm_ref[:, lo + LANES:lo + 2 * LANES] = jnp.where(own, q_rope[h // 2], 0.0).astype(bf)
        km_ref[:, lo:lo + LANES] = kv[:, lo:lo + LANES].astype(bf)
        km_ref[:, lo + LANES:lo + 2 * LANES] = kpe
        vm_ref[:, LANES * h:LANES * (h + 1)] = kv[:, lo + LANES:lo + 2 * LANES].astype(bf)

    cg, sag, sbg = cg_ref[...], sag_ref[...], sbg_ref[...]
    scale_g = GQA_HEAD_DIM ** -0.5
    for h in range(GQA_HEADS):
        t = _rms(z[:, Z_GQ + LANES * h:Z_GQ + LANES * (h + 1)], ggq_ref[...])
        qg_ref[:, LANES * h:LANES * (h + 1)] = (_rope(t, cg, sag, sbg) * scale_g).astype(bf)
    for h in range(GQA_KV_HEADS):
        t = _rms(z[:, Z_GK + LANES * h:Z_GK + LANES * (h + 1)], ggk_ref[...])
        kg_ref[:, LANES * h:LANES * (h + 1)] = _rope(t, cg, sag, sbg).astype(bf)
    vg_ref[...] = z[:, Z_GV:Z_WIDTH].astype(bf)


def _projections(x2, g1, win, gq, wq, gkv, wkv, ggq, ggk, tables, batch, seq, tm):
    tokens, d_model = x2.shape
    ns = seq // tm
    row = lambda si, b: (b * ns + si, 0)
    const = lambda si, b: (0, 0)
    tab = lambda si, b: (si, 0)

    def full(a):
        return pl.BlockSpec(a.shape, const)

    bf = jnp.bfloat16
    widths = (2 * LANES * MLA_HEADS, 2 * LANES * MLA_HEADS, MLA_HEADS * MLA_V,
              GQA_HEADS * GQA_HEAD_DIM, GQA_KV_HEADS * GQA_HEAD_DIM, GQA_KV_HEADS * GQA_HEAD_DIM)
    return pl.pallas_call(
        _proj_kernel,
        grid=(ns, batch),
        in_specs=[pl.BlockSpec((tm, d_model), row), full(g1), full(win), full(gq), full(wq),
                  full(gkv), full(wkv), full(ggq), full(ggk)]
                 + [pl.BlockSpec((tm, LANES), tab) for _ in tables],
        out_specs=[pl.BlockSpec((tm, w), row) for w in widths],
        out_shape=[jax.ShapeDtypeStruct((tokens, w), bf) for w in widths],
        compiler_params=pltpu.CompilerParams(
            dimension_semantics=("arbitrary", "arbitrary"), vmem_limit_bytes=VMEM_LIMIT),
        name="proj",
    )(x2, g1, win, gq, wq, gkv, wkv, ggq, ggk, *tables)


def _softmax_pv(s, v):
    m = jnp.max(s, axis=-1, keepdims=True)
    p = jnp.exp(s - m)
    l = jnp.sum(p, axis=-1, keepdims=True)
    return _dot(p.astype(jnp.bfloat16), v) / l


def _qk(q, k):
    return lax.dot_general(q, k, (((1,), (1,)), ((), ())), preferred_element_type=jnp.float32)


def _attn_kernel(qm_ref, qg_ref, km_ref, vm_ref, kg_ref, vg_ref, x_ref, gnm_ref, gng_ref, wo_ref,
                 h_ref):
    tq = qm_ref.shape[0]
    o_mla = []
    for h in range(MLA_HEADS):
        lo = 2 * LANES * h
        s = _qk(qm_ref[:, lo:lo + 2 * LANES], km_ref[:, lo:lo + 2 * LANES])
        o_mla.append(_softmax_pv(s, vm_ref[:, MLA_V * h:MLA_V * (h + 1)]))
    o_gqa = []
    group = GQA_HEADS // GQA_KV_HEADS
    for j in range(GQA_KV_HEADS):
        q = jnp.concatenate([qg_ref[:, LANES * (group * j + g):LANES * (group * j + g + 1)]
                             for g in range(group)], axis=0)
        s = _qk(q, kg_ref[:, LANES * j:LANES * (j + 1)])
        o = _softmax_pv(s, vg_ref[:, LANES * j:LANES * (j + 1)])
        o_gqa.extend(o[tq * g:tq * (g + 1)] for g in range(group))
    o = jnp.concatenate([_rms(jnp.concatenate(o_mla, axis=-1), gnm_ref[...]),
                         _rms(jnp.concatenate(o_gqa, axis=-1), gng_ref[...])], axis=-1)
    h_ref[...] = x_ref[...] + _dot(o.astype(jnp.bfloat16), wo_ref[...])


def _attention(qm, qg, km, vm, kg, vg, x2, gnm, gng, wo, batch, seq, tq):
    tokens, d_model = x2.shape
    nq = seq // tq
    row = lambda b, qi: (b * nq + qi, 0)
    per_batch = lambda b, qi: (b, 0)
    const = lambda b, qi: (0, 0)

    def full(a):
        return pl.BlockSpec(a.shape, const)

    return pl.pallas_call(
        _attn_kernel,
        grid=(batch, nq),
        in_specs=[pl.BlockSpec((tq, qm.shape[1]), row), pl.BlockSpec((tq, qg.shape[1]), row),
                  pl.BlockSpec((seq, km.shape[1]), per_batch),
                  pl.BlockSpec((seq, vm.shape[1]), per_batch),
                  pl.BlockSpec((seq, kg.shape[1]), per_batch),
                  pl.BlockSpec((seq, vg.shape[1]), per_batch),
                  pl.BlockSpec((tq, d_model), row), full(gnm), full(gng), full(wo)],
        out_specs=pl.BlockSpec((tq, d_model), row),
        out_shape=jax.ShapeDtypeStruct((tokens, d_model), jnp.float32),
        compiler_params=pltpu.CompilerParams(
            dimension_semantics=("arbitrary", "arbitrary"), vmem_limit_bytes=VMEM_LIMIT),
        name="attn",
    )(qm, qg, km, vm, kg, vg, x2, gnm, gng, wo)


def _ffn_kernel(h_ref, hp_ref, hn_ref, g2_ref, wup_ref, cw_ref, cb_ref, wdn_ref, gf_ref,
                o_ref, u_ref, acc_ref, *, tiles_per_seq, final_norm):
    tm = h_ref.shape[0]
    bf = jnp.bfloat16
    i = pl.program_id(0)
    si = i % tiles_per_seq
    keep_prev = (si != 0).astype(jnp.float32)
    keep_next = (si != tiles_per_seq - 1).astype(jnp.float32)
    g2 = g2_ref[...]
    u_ref[0:HALO, :] = (_rms(hp_ref[...], g2) * keep_prev).astype(bf)
    u_ref[HALO:HALO + tm, :] = _rms(h_ref[...], g2).astype(bf)
    u_ref[HALO + tm:, :] = (_rms(hn_ref[...], g2) * keep_next).astype(bf)
    acc_ref[...] = jnp.zeros_like(acc_ref)

    def conv(t, w, b):
        return (t[HALO - 1:HALO - 1 + tm] * w[0:1] + t[HALO:HALO + tm] * w[1:2]
                + t[HALO + 1:HALO + 1 + tm] * w[2:3] + b)

    def chunk(c, carry):
        u = u_ref[...]
        gate = conv(_dot(u, wup_ref[c]), cw_ref[c], cb_ref[c])
        val = conv(_dot(u, wup_ref[N_FF_CHUNKS + c]), cw_ref[N_FF_CHUNKS + c],
                   cb_ref[N_FF_CHUNKS + c])
        act = (gate * jax.nn.sigmoid(gate) * val).astype(bf)
        acc_ref[...] += _dot(act, wdn_ref[c])
        return carry

    lax.fori_loop(0, N_FF_CHUNKS, chunk, 0)
    out = h_ref[...] + acc_ref[...]
    if final_norm:
        out = _rms(out, gf_ref[...])
    o_ref[...] = out


def _conv_ffn(h2, g2, wup, cw, cb, wdn, gf, seq, tm, final_norm):
    tokens, d_model = h2.shape
    tiles_per_seq = seq // tm
    n_halo_blocks = tokens // HALO
    per_tile = tm // HALO
    const2 = lambda i: (0, 0)
    const3 = lambda i: (0, 0, 0)
    kern = functools.partial(_ffn_kernel, tiles_per_seq=tiles_per_seq, final_norm=final_norm)
    return pl.pallas_call(
        kern,
        grid=(tokens // tm,),
        in_specs=[pl.BlockSpec((tm, d_model), lambda i: (i, 0)),
                  pl.BlockSpec((HALO, d_model), lambda i: (jnp.maximum(i * per_tile - 1, 0), 0)),
                  pl.BlockSpec((HALO, d_model),
                               lambda i: (jnp.minimum((i + 1) * per_tile, n_halo_blocks - 1), 0)),
                  pl.BlockSpec(g2.shape, const2),
                  pl.BlockSpec(wup.shape, const3, pipeline_mode=pl.Buffered(1)),
                  pl.BlockSpec(cw.shape, const3), pl.BlockSpec(cb.shape, const3),
                  pl.BlockSpec(wdn.shape, const3, pipeline_mode=pl.Buffered(1)),
                  pl.BlockSpec(gf.shape, const2)],
        out_specs=pl.BlockSpec((tm, d_model), lambda i: (i, 0)),
        out_shape=jax.ShapeDtypeStruct((tokens, d_model), jnp.float32),
        scratch_shapes=[pltpu.VMEM((tm + 2 * HALO, d_model), jnp.bfloat16),
                        pltpu.VMEM((tm, d_model), jnp.float32)],
        compiler_params=pltpu.CompilerParams(
            dimension_semantics=("arbitrary",), vmem_limit_bytes=VMEM_LIMIT),
        name="convffn",
    )(h2, h2, h2, g2, wup, cw, cb, wdn, gf)


def _rope_tables(pos, dim):
    inv = ROPE_THETA ** (-jnp.arange(0, dim, 2, dtype=jnp.float32) / dim)
    ang = pos.astype(jnp.float32)[:, None] * inv[None, :]
    return jnp.cos(ang), jnp.sin(ang)


def _lane_tables(cos_a, sin_a, cos_b, sin_b):
    zero = jnp.zeros_like(sin_a)
    c = jnp.concatenate([cos_a, cos_a, cos_b, cos_b], axis=-1)
    sa = jnp.concatenate([-sin_a, zero, -sin_b, zero], axis=-1)
    sb = jnp.concatenate([zero, sin_a, zero, sin_b], axis=-1)
    return c, sa, sb


def _pick_tile(seq, want):
    t = min(seq, want)
    assert seq % t == 0 and t % HALO == 0
    return t


def kernel(x, norm1_g, w_in, mla_q_norm_g, mla_w_q_up, mla_kv_norm_g, mla_w_kv_up, gqa_q_norm_g,
           gqa_k_norm_g, group_norm_mla_g, group_norm_gqa_g, w_out, norm2_g, ffn_w_up, ffn_conv_w,
           ffn_conv_b, ffn_w_down, final_norm_g):
    batch, seq, d_model = x.shape
    depth = w_in.shape[0]
    bf = jnp.bfloat16
    tm_proj = _pick_tile(seq, 512)
    tq = _pick_tile(seq, 256)
    tm_ffn = _pick_tile(seq, 512)

    t = jnp.arange(seq, dtype=jnp.int32)
    cos1, sin1 = _rope_tables(t, MLA_ROPE)
    cos_r, sin_r = _rope_tables(t // GRID_W, GQA_HEAD_DIM // 2)
    cos_c, sin_c = _rope_tables(t % GRID_W, GQA_HEAD_DIM // 2)
    tables = _lane_tables(cos1, sin1, cos1, sin1) + _lane_tables(cos_r, sin_r, cos_c, sin_c)

    qk = MLA_QK
    q_cols = ([qk * h + j for h in range(MLA_HEADS) for j in range(MLA_NOPE)]
              + [qk * h + MLA_NOPE + j for h in range(MLA_HEADS) for j in range(MLA_ROPE)])
    q_cols = jnp.asarray(q_cols, dtype=jnp.int32)
    kpe_lo = MLA_Q_RANK + MLA_KV_RANK
    kpe_hi = kpe_lo + MLA_ROPE

    row = lambda v: v.reshape(1, -1)
    h2 = x.reshape(batch * seq, d_model)
    for l in range(depth):
        win = jnp.concatenate([w_in[l][:, :kpe_hi], w_in[l][:, kpe_lo:kpe_hi], w_in[l][:, kpe_hi:]],
                              axis=1).astype(bf)
        wq = jnp.take(mla_w_q_up[l], q_cols, axis=1).astype(bf)
        qm, km, vm, qg, kg, vg = _projections(
            h2, row(norm1_g[l]), win, row(mla_q_norm_g[l]), wq, row(mla_kv_norm_g[l]),
            mla_w_kv_up[l].astype(bf), row(gqa_q_norm_g[l]), row(gqa_k_norm_g[l]), tables,
            batch, seq, tm_proj)
        h2 = _attention(qm, qg, km, vm, kg, vg, h2, row(group_norm_mla_g[l]),
                        row(group_norm_gqa_g[l]), w_out[l].astype(bf), batch, seq, tq)
        wup = ffn_w_up[l].astype(bf).reshape(d_model, 2 * N_FF_CHUNKS, FF_CHUNK).transpose(1, 0, 2)
        cw = ffn_conv_w[l].reshape(3, 2 * N_FF_CHUNKS, FF_CHUNK).transpose(1, 0, 2)
        cb = ffn_conv_b[l].reshape(2 * N_FF_CHUNKS, 1, FF_CHUNK)
        wdn = ffn_w_down[l].astype(bf).reshape(N_FF_CHUNKS, FF_CHUNK, d_model)
        h2 = _conv_ffn(h2, row(norm2_g[l]), wup, cw, cb, wdn, row(final_norm_g), seq, tm_ffn,
                       final_norm=(l == depth - 1))
    return h2.reshape(batch, seq, d_model)
```

```python
import functools

import jax
import jax.numpy as jnp
from jax import lax
from jax.experimental import pallas as pl
from jax.experimental.pallas import tpu as pltpu

EPS = 1e-6
LOG2_E = 1.4426950408889634
ROPE_THETA = 10000.0
GRID_W = 64

MLA_HEADS = 4
MLA_Q_RANK = 384
MLA_KV_RANK = 256
MLA_NOPE = 128
MLA_ROPE = 64
MLA_V = 128
MLA_QK = MLA_NOPE + MLA_ROPE

GQA_HEADS = 4
GQA_KV_HEADS = 2
GQA_HEAD_DIM = 128

D_FF = 2816
LANES = 128

Z_CQ = 0
Z_CKV = Z_CQ + MLA_Q_RANK
Z_KPE = Z_CKV + MLA_KV_RANK
Z_GQ = Z_KPE + 2 * MLA_ROPE
Z_GK = Z_GQ + GQA_HEADS * GQA_HEAD_DIM
Z_GV = Z_GK + GQA_KV_HEADS * GQA_HEAD_DIM
Z_WIDTH = Z_GV + GQA_KV_HEADS * GQA_HEAD_DIM

FF_CHUNK = 256
N_FF_CHUNKS = D_FF // FF_CHUNK
HALO = 16

VMEM_LIMIT = 56 * 1024 * 1024


def _rms(x, g):
    y = x * lax.rsqrt(jnp.mean(x * x, axis=-1, keepdims=True) + EPS)
    return y * g


def _rope(x, c, sa, sb):
    return x * c + pltpu.roll(x, 96, 1) * sa + pltpu.roll(x, 32, 1) * sb


def _dot(a, b):
    return jnp.dot(a, b, preferred_element_type=jnp.float32)


def _proj_kernel(x_ref, g1_ref, win_ref, gq_ref, wq_ref, gkv_ref, wkv_ref, ggq_ref, ggk_ref,
                 c1_ref, sa1_ref, sb1_ref, cg_ref, sag_ref, sbg_ref,
                 qm_ref, km_ref, vm_ref, qg_ref, kg_ref, vg_ref):
    bf = jnp.bfloat16
    u = _rms(x_ref[...], g1_ref[...]).astype(bf)
    z = _dot(u, win_ref[...])

    c_q = _rms(z[:, Z_CQ:Z_CKV], gq_ref[...]).astype(bf)
    q = _dot(c_q, wq_ref[...])
    c_kv = _rms(z[:, Z_CKV:Z_KPE], gkv_ref[...]).astype(bf)
    kv = _dot(c_kv, wkv_ref[...])

    c1, sa1, sb1 = c1_ref[...], sa1_ref[...], sb1_ref[...]
    kpe = _rope(z[:, Z_KPE:Z_GQ], c1, sa1, sb1).astype(bf)
    scale_m = MLA_QK ** -0.5 * LOG2_E
    ones = jnp.ones((x_ref.shape[0], LANES), bf)
    nope_w = MLA_HEADS * MLA_NOPE
    q_rope = [_rope(q[:, nope_w + LANES * i:nope_w + LANES * (i + 1)], c1, sa1, sb1) * scale_m
              for i in range(MLA_HEADS // 2)]
    lane = lax.broadcasted_iota(jnp.int32, (1, LANES), 1)
    for h in range(MLA_HEADS):
        lo = 2 * LANES * h
        qm_ref[:, lo:lo + LANES] = (q[:, LANES * h:LANES * (h + 1)] * scale_m).astype(bf)
        own = (lane < MLA_ROPE) if h % 2 == 0 else (lane >= MLA_ROPE)
        qm_ref[:, lo + LANES:lo + 2 * LANES] = jnp.where(own, q_rope[h // 2], 0.0).astype(bf)
        km_ref[:, lo:lo + LANES] = kv[:, lo:lo + LANES].astype(bf)
        km_ref[:, lo + LANES:lo + 2 * LANES] = kpe
        vm_ref[:, lo:lo + LANES] = kv[:, lo + LANES:lo + 2 * LANES].astype(bf)
        vm_ref[:, lo + LANES:lo + 2 * LANES] = ones

    cg, sag, sbg = cg_ref[...], sag_ref[...], sbg_ref[...]
    scale_g = GQA_HEAD_DIM ** -0.5 * LOG2_E
    for h in range(GQA_HEADS):
        t = _rms(z[:, Z_GQ + LANES * h:Z_GQ + LANES * (h + 1)], ggq_ref[...])
        qg_ref[:, LANES * h:LANES * (h + 1)] = (_rope(t, cg, sag, sbg) * scale_g).astype(bf)
    for h in range(GQA_KV_HEADS):
        t = _rms(z[:, Z_GK + LANES * h:Z_GK + LANES * (h + 1)], ggk_ref[...])
        kg_ref[:, LANES * h:LANES * (h + 1)] = _rope(t, cg, sag, sbg).astype(bf)
        lo = 2 * LANES * h
        vg_ref[:, lo:lo + LANES] = z[:, Z_GV + LANES * h:Z_GV + LANES * (h + 1)].astype(bf)
        vg_ref[:, lo + LANES:lo + 2 * LANES] = ones


def _projections(x2, g1, win, gq, wq, gkv, wkv, ggq, ggk, tables, batch, seq, tm):
    tokens, d_model = x2.shape
    ns = seq // tm
    row = lambda si, b: (b * ns + si, 0)
    const = lambda si, b: (0, 0)
    tab = lambda si, b: (si, 0)

    def full(a):
        return pl.BlockSpec(a.shape, const)

    bf = jnp.bfloat16
    widths = (2 * LANES * MLA_HEADS, 2 * LANES * MLA_HEADS, 2 * LANES * MLA_HEADS,
              GQA_HEADS * GQA_HEAD_DIM, GQA_KV_HEADS * GQA_HEAD_DIM, 2 * LANES * GQA_KV_HEADS)
    return pl.pallas_call(
        _proj_kernel,
        grid=(ns, batch),
        in_specs=[pl.BlockSpec((tm, d_model), row), full(g1), full(win), full(gq), full(wq),
                  full(gkv), full(wkv), full(ggq), full(ggk)]
                 + [pl.BlockSpec((tm, LANES), tab) for _ in tables],
        out_specs=[pl.BlockSpec((tm, w), row) for w in widths],
        out_shape=[jax.ShapeDtypeStruct((tokens, w), bf) for w in widths],
        compiler_params=pltpu.CompilerParams(
            dimension_semantics=("arbitrary", "arbitrary"), vmem_limit_bytes=VMEM_LIMIT),
        name="proj",
    )(x2, g1, win, gq, wq, gkv, wkv, ggq, ggk, *tables)


def _softmax_pv(s, v_ones):
    m = jnp.max(s, axis=-1, keepdims=True)
    p = jnp.exp2(s - m).astype(jnp.bfloat16)
    o = _dot(p, v_ones)
    return o[:, :LANES] / o[:, LANES:]


def _qk(q, k):
    return lax.dot_general(q, k, (((1,), (1,)), ((), ())), preferred_element_type=jnp.float32)


def _attn_kernel(qm_ref, qg_ref, km_ref, vm_ref, kg_ref, vg_ref, x_ref, gnm_ref, gng_ref, wo_ref,
                 h_ref):
    tq = qm_ref.shape[0]
    o_mla = []
    for h in range(MLA_HEADS):
        lo = 2 * LANES * h
        s = _qk(qm_ref[:, lo:lo + 2 * LANES], km_ref[:, lo:lo + 2 * LANES])
        o_mla.append(_softmax_pv(s, vm_ref[:, lo:lo + 2 * LANES]))
    o_gqa = []
    group = GQA_HEADS // GQA_KV_HEADS
    for j in range(GQA_KV_HEADS):
        q = jnp.concatenate([qg_ref[:, LANES * (group * j + g):LANES * (group * j + g + 1)]
                             for g in range(group)], axis=0)
        s = _qk(q, kg_ref[:, LANES * j:LANES * (j + 1)])
        o = _softmax_pv(s, vg_ref[:, 2 * LANES * j:2 * LANES * (j + 1)])
        o_gqa.extend(o[tq * g:tq * (g + 1)] for g in range(group))
    o = jnp.concatenate([_rms(jnp.concatenate(o_mla, axis=-1), gnm_ref[...]),
                         _rms(jnp.concatenate(o_gqa, axis=-1), gng_ref[...])], axis=-1)
    h_ref[...] = x_ref[...] + _dot(o.astype(jnp.bfloat16), wo_ref[...])


def _attention(qm, qg, km, vm, kg, vg, x2, gnm, gng, wo, batch, seq, tq):
    tokens, d_model = x2.shape
    nq = seq // tq
    row = lambda b, qi: (b * nq + qi, 0)
    per_batch = lambda b, qi: (b, 0)
    const = lambda b, qi: (0, 0)

    def full(a):
        return pl.BlockSpec(a.shape, const)

    return pl.pallas_call(
        _attn_kernel,
        grid=(batch, nq),
        in_specs=[pl.BlockSpec((tq, qm.shape[1]), row), pl.BlockSpec((tq, qg.shape[1]), row),
                  pl.BlockSpec((seq, km.shape[1]), per_batch),
                  pl.BlockSpec((seq, vm.shape[1]), per_batch),
                  pl.BlockSpec((seq, kg.shape[1]), per_batch),
                  pl.BlockSpec((seq, vg.shape[1]), per_batch),
                  pl.BlockSpec((tq, d_model), row), full(gnm), full(gng), full(wo)],
        out_specs=pl.BlockSpec((tq, d_model), row),
        out_shape=jax.ShapeDtypeStruct((tokens, d_model), jnp.float32),
        compiler_params=pltpu.CompilerParams(
            dimension_semantics=("arbitrary", "arbitrary"), vmem_limit_bytes=VMEM_LIMIT),
        name="attn",
    )(qm, qg, km, vm, kg, vg, x2, gnm, gng, wo)


def _ffn_kernel(h_ref, hp_ref, hn_ref, g2_ref, wup_ref, cw_ref, cb_ref, wdn_ref, gf_ref,
                o_ref, u_ref, hu_ref, act_ref, *, tiles_per_seq, final_norm):
    tm = h_ref.shape[0]
    bf = jnp.bfloat16
    i = pl.program_id(0)
    si = i % tiles_per_seq
    keep_prev = (si != 0).astype(jnp.float32)
    keep_next = (si != tiles_per_seq - 1).astype(jnp.float32)
    g2 = g2_ref[...]
    u_ref[0:HALO, :] = (_rms(hp_ref[...], g2) * keep_prev).astype(bf)
    u_ref[HALO:HALO + tm, :] = _rms(h_ref[...], g2).astype(bf)
    u_ref[HALO + tm:, :] = (_rms(hn_ref[...], g2) * keep_next).astype(bf)
    lane_tiles = FF_CHUNK // LANES

    def up(c):
        u = u_ref[...]
        for part in range(2):
            t = _dot(u, wup_ref[part * N_FF_CHUNKS + c])
            for j in range(lane_tiles):
                hu_ref[c % 2, part, j] = t[:, LANES * j:LANES * (j + 1)]

    def conv(c, part, j):
        w = cw_ref[part * N_FF_CHUNKS + c][:, LANES * j:LANES * (j + 1)]
        b = cb_ref[part * N_FF_CHUNKS + c][:, LANES * j:LANES * (j + 1)]
        src = hu_ref.at[c % 2, part, j]
        return (src[pl.ds(HALO - 1, tm), :] * w[0:1] + src[pl.ds(HALO, tm), :] * w[1:2]
                + src[pl.ds(HALO + 1, tm), :] * w[2:3] + b)

    def act(c):
        for j in range(lane_tiles):
            gate = conv(c, 0, j)
            lo = FF_CHUNK * c + LANES * j
            act_ref[:, lo:lo + LANES] = (gate * jax.nn.sigmoid(gate) * conv(c, 1, j)).astype(bf)

    up(0)
    for c in range(N_FF_CHUNKS):
        if c + 1 < N_FF_CHUNKS:
            up(c + 1)
        act(c)
    out = h_ref[...] + _dot(act_ref[...], wdn_ref[...])
    if final_norm:
        out = _rms(out, gf_ref[...])
    o_ref[...] = out


def _conv_ffn(h2, g2, wup, cw, cb, wdn, gf, seq, tm, final_norm):
    tokens, d_model = h2.shape
    tiles_per_seq = seq // tm
    n_halo_blocks = tokens // HALO
    per_tile = tm // HALO
    const2 = lambda i: (0, 0)
    const3 = lambda i: (0, 0, 0)
    kern = functools.partial(_ffn_kernel, tiles_per_seq=tiles_per_seq, final_norm=final_norm)
    return pl.pallas_call(
        kern,
        grid=(tokens // tm,),
        in_specs=[pl.BlockSpec((tm, d_model), lambda i: (i, 0)),
                  pl.BlockSpec((HALO, d_model), lambda i: (jnp.maximum(i * per_tile - 1, 0), 0)),
                  pl.BlockSpec((HALO, d_model),
                               lambda i: (jnp.minimum((i + 1) * per_tile, n_halo_blocks - 1), 0)),
                  pl.BlockSpec(g2.shape, const2),
                  pl.BlockSpec(wup.shape, const3, pipeline_mode=pl.Buffered(1)),
                  pl.BlockSpec(cw.shape, const3), pl.BlockSpec(cb.shape, const3),
                  pl.BlockSpec(wdn.shape, const2, pipeline_mode=pl.Buffered(1)),
                  pl.BlockSpec(gf.shape, const2)],
        out_specs=pl.BlockSpec((tm, d_model), lambda i: (i, 0)),
        out_shape=jax.ShapeDtypeStruct((tokens, d_model), jnp.float32),
        scratch_shapes=[pltpu.VMEM((tm + 2 * HALO, d_model), jnp.bfloat16),
                        pltpu.VMEM((2, 2, FF_CHUNK // LANES, tm + 2 * HALO, LANES), jnp.float32),
                        pltpu.VMEM((tm, D_FF), jnp.bfloat16)],
        compiler_params=pltpu.CompilerParams(
            dimension_semantics=("arbitrary",), vmem_limit_bytes=VMEM_LIMIT),
        name="convffn",
    )(h2, h2, h2, g2, wup, cw, cb, wdn, gf)


def _rope_tables(pos, dim):
    inv = ROPE_THETA ** (-jnp.arange(0, dim, 2, dtype=jnp.float32) / dim)
    ang = pos.astype(jnp.float32)[:, None] * inv[None, :]
    return jnp.cos(ang), jnp.sin(ang)


def _lane_tables(cos_a, sin_a, cos_b, sin_b):
    zero = jnp.zeros_like(sin_a)
    c = jnp.concatenate([cos_a, cos_a, cos_b, cos_b], axis=-1)
    sa = jnp.concatenate([-sin_a, zero, -sin_b, zero], axis=-1)
    sb = jnp.concatenate([zero, sin_a, zero, sin_b], axis=-1)
    return c, sa, sb


def _pick_tile(seq, want):
    t = min(seq, want)
    assert seq % t == 0 and t % HALO == 0
    return t


def kernel(x, norm1_g, w_in, mla_q_norm_g, mla_w_q_up, mla_kv_norm_g, mla_w_kv_up, gqa_q_norm_g,
           gqa_k_norm_g, group_norm_mla_g, group_norm_gqa_g, w_out, norm2_g, ffn_w_up, ffn_conv_w,
           ffn_conv_b, ffn_w_down, final_norm_g):
    batch, seq, d_model = x.shape
    depth = w_in.shape[0]
    bf = jnp.bfloat16
    tm_proj = _pick_tile(seq, 512)
    tq = _pick_tile(seq, 256)
    tm_ffn = _pick_tile(seq, 512)

    t = jnp.arange(seq, dtype=jnp.int32)
    cos1, sin1 = _rope_tables(t, MLA_ROPE)
    cos_r, sin_r = _rope_tables(t // GRID_W, GQA_HEAD_DIM // 2)
    cos_c, sin_c = _rope_tables(t % GRID_W, GQA_HEAD_DIM // 2)
    tables = _lane_tables(cos1, sin1, cos1, sin1) + _lane_tables(cos_r, sin_r, cos_c, sin_c)

    qk = MLA_QK
    q_cols = ([qk * h + j for h in range(MLA_HEADS) for j in range(MLA_NOPE)]
              + [qk * h + MLA_NOPE + j for h in range(MLA_HEADS) for j in range(MLA_ROPE)])
    q_cols = jnp.asarray(q_cols, dtype=jnp.int32)
    kpe_lo = MLA_Q_RANK + MLA_KV_RANK
    kpe_hi = kpe_lo + MLA_ROPE

    row = lambda v: v.reshape(1, -1)
    h2 = x.reshape(batch * seq, d_model)
    for l in range(depth):
        win = jnp.concatenate([w_in[l][:, :kpe_hi], w_in[l][:, kpe_lo:kpe_hi], w_in[l][:, kpe_hi:]],
                              axis=1).astype(bf)
        wq = jnp.take(mla_w_q_up[l], q_cols, axis=1).astype(bf)
        qm, km, vm, qg, kg, vg = _projections(
            h2, row(norm1_g[l]), win, row(mla_q_norm_g[l]), wq, row(mla_kv_norm_g[l]),
            mla_w_kv_up[l].astype(bf), row(gqa_q_norm_g[l]), row(gqa_k_norm_g[l]), tables,
            batch, seq, tm_proj)
        h2 = _attention(qm, qg, km, vm, kg, vg, h2, row(group_norm_mla_g[l]),
                        row(group_norm_gqa_g[l]), w_out[l].astype(bf), batch, seq, tq)
        wup = ffn_w_up[l].astype(bf).reshape(d_model, 2 * N_FF_CHUNKS, FF_CHUNK).transpose(1, 0, 2)
        cw = ffn_conv_w[l].reshape(3, 2 * N_FF_CHUNKS, FF_CHUNK).transpose(1, 0, 2)
        cb = ffn_conv_b[l].reshape(2 * N_FF_CHUNKS, 1, FF_CHUNK)
        wdn = ffn_w_down[l].astype(bf)
        h2 = _conv_ffn(h2, row(norm2_g[l]), wup, cw, cb, wdn, row(final_norm_g), seq, tm_ffn,
                       final_norm=(l == depth - 1))
    return h2.reshape(batch, seq, d_model)
```

```python
import functools

import numpy as np

import jax
import jax.numpy as jnp
from jax import lax
from jax.experimental import pallas as pl
from jax.experimental.pallas import tpu as pltpu

EPS = 1e-6
LOG2_E = 1.4426950408889634
ROPE_THETA = 10000.0
GRID_W = 64

D_MODEL = 1024
MLA_HEADS = 4
MLA_Q_RANK = 384
MLA_KV_RANK = 256
MLA_NOPE = 128
MLA_ROPE = 64
MLA_V = 128
MLA_QK = MLA_NOPE + MLA_ROPE

GQA_HEADS = 4
GQA_KV_HEADS = 2
GQA_HEAD_DIM = 128
GQA_GROUP = GQA_HEADS // GQA_KV_HEADS

D_FF = 2816
LANES = 128
BF16_ROWS = 16

OFF_KPE = MLA_Q_RANK + MLA_KV_RANK
OFF_GQ = OFF_KPE + MLA_ROPE
OFF_GK = OFF_GQ + GQA_HEADS * GQA_HEAD_DIM
OFF_GV = OFF_GK + GQA_KV_HEADS * GQA_HEAD_DIM
IN_WIDTH = OFF_GV + GQA_KV_HEADS * GQA_HEAD_DIM

Z_CQ = 0
Z_CKV = Z_CQ + MLA_Q_RANK
Z_KPE = Z_CKV + MLA_KV_RANK
Z_GQ = Z_KPE + LANES
Z_GK = Z_GQ + GQA_HEADS * GQA_HEAD_DIM
Z_WIDTH = Z_GK + GQA_KV_HEADS * GQA_HEAD_DIM

N_V_HEADS = MLA_HEADS + GQA_KV_HEADS
VT_ROWS = MLA_V + BF16_ROWS

FF_CHUNK = 256
N_FF_CHUNKS = D_FF // FF_CHUNK
HALO = 16
KEY_CHUNK = 512
PROJ_SPLIT = 2

VMEM_LIMIT = 56 * 1024 * 1024


def _rms(x, g):
    y = x * lax.rsqrt(jnp.mean(x * x, axis=-1, keepdims=True) + EPS)
    return y * g


def _rope(x, c, s):
    return x * c + pltpu.roll(x, LANES // 2, 1) * s


def _dot(a, b):
    return jnp.dot(a, b, preferred_element_type=jnp.float32)


def _dot_t(a, b):
    return lax.dot_general(a, b, (((1,), (1,)), ((), ())), preferred_element_type=jnp.float32)


def _proj_kernel(x_ref, g1_ref, win_ref, winvt_ref, gq_ref, wq_ref, gkv_ref, wkvk_ref, wkvvt_ref,
                 ggq_ref, ggk_ref, c1_ref, s1_ref, cg_ref, sg_ref,
                 qm_ref, km_ref, qg_ref, kg_ref, vt_ref):
    bf = jnp.bfloat16
    tm = x_ref.shape[0]
    rows_per = tm // PROJ_SPLIT
    scale_m = MLA_QK ** -0.5 * LOG2_E
    scale_g = GQA_HEAD_DIM ** -0.5 * LOG2_E
    nope_w = MLA_HEADS * MLA_NOPE
    lane = lax.broadcasted_iota(jnp.int32, (1, LANES), 1)
    first_of_pair = (lane % (LANES // 2)) < (MLA_ROPE // 2)

    for r in range(PROJ_SPLIT):
        rows = pl.ds(r * rows_per, rows_per)
        cols = slice(r * rows_per, (r + 1) * rows_per)
        u = _rms(x_ref[rows, :], g1_ref[...]).astype(bf)
        z = _dot(u, win_ref[...])
        c_q = _rms(z[:, Z_CQ:Z_CKV], gq_ref[...]).astype(bf)
        q = _dot(c_q, wq_ref[...])
        c_kv = _rms(z[:, Z_CKV:Z_KPE], gkv_ref[...]).astype(bf)
        k_nope = _dot(c_kv, wkvk_ref[...])
        vt_mla = _dot_t(wkvvt_ref[...], c_kv)
        vt_gqa = _dot_t(winvt_ref[...], u)

        c1, s1 = c1_ref[rows, :], s1_ref[rows, :]
        kpe = _rope(z[:, Z_KPE:Z_GQ], c1, s1).astype(bf)
        q_rope = [_rope(q[:, nope_w + LANES * i:nope_w + LANES * (i + 1)], c1, s1) * scale_m
                  for i in range(MLA_HEADS // 2)]
        for h in range(MLA_HEADS):
            lo = 2 * LANES * h
            qm_ref[rows, lo:lo + LANES] = (q[:, LANES * h:LANES * (h + 1)] * scale_m).astype(bf)
            own = first_of_pair if h % 2 == 0 else jnp.logical_not(first_of_pair)
            qm_ref[rows, lo + LANES:lo + 2 * LANES] = jnp.where(own, q_rope[h // 2], 0.0).astype(bf)
            km_ref[rows, lo:lo + LANES] = k_nope[:, LANES * h:LANES * (h + 1)].astype(bf)
            km_ref[rows, lo + LANES:lo + 2 * LANES] = kpe
            vt_ref[VT_ROWS * h:VT_ROWS * h + MLA_V, cols] = vt_mla[MLA_V * h:MLA_V * (h + 1)].astype(bf)

        cg, sg = cg_ref[rows, :], sg_ref[rows, :]
        for h in range(GQA_HEADS):
            t = _rms(z[:, Z_GQ + LANES * h:Z_GQ + LANES * (h + 1)], ggq_ref[...])
            qg_ref[rows, LANES * h:LANES * (h + 1)] = (_rope(t, cg, sg) * scale_g).astype(bf)
        for j in range(GQA_KV_HEADS):
            t = _rms(z[:, Z_GK + LANES * j:Z_GK + LANES * (j + 1)], ggk_ref[...])
            kg_ref[rows, LANES * j:LANES * (j + 1)] = _rope(t, cg, sg).astype(bf)
            lo = VT_ROWS * (MLA_HEADS + j)
            vt_ref[lo:lo + MLA_V, cols] = vt_gqa[LANES * j:LANES * (j + 1)].astype(bf)

    ones = jnp.ones((BF16_ROWS, tm), bf)
    for h in range(N_V_HEADS):
        vt_ref[VT_ROWS * h + MLA_V:VT_ROWS * (h + 1), :] = ones


def _projections(x2, g1, win, winvt, gq, wq, gkv, wkvk, wkvvt, ggq, ggk, tables, batch, seq, tm):
    tokens, d_model = x2.shape
    ns = seq // tm
    row = lambda si, b: (b * ns + si, 0)
    const = lambda si, b: (0, 0)
    tab = lambda si, b: (si, 0)

    def full(a):
        return pl.BlockSpec(a.shape, const)

    bf = jnp.bfloat16
    widths = (2 * LANES * MLA_HEADS, 2 * LANES * MLA_HEADS,
              GQA_HEADS * GQA_HEAD_DIM, GQA_KV_HEADS * GQA_HEAD_DIM)
    vt_rows = N_V_HEADS * VT_ROWS
    return pl.pallas_call(
        _proj_kernel,
        grid=(ns, batch),
        in_specs=[pl.BlockSpec((tm, d_model), row), full(g1), full(win), full(winvt), full(gq),
                  full(wq), full(gkv), full(wkvk), full(wkvvt), full(ggq), full(ggk)]
                 + [pl.BlockSpec((tm, LANES), tab) for _ in tables],
        out_specs=[pl.BlockSpec((tm, w), row) for w in widths]
                  + [pl.BlockSpec((None, vt_rows, tm), lambda si, b: (b, 0, si))],
        out_shape=[jax.ShapeDtypeStruct((tokens, w), bf) for w in widths]
                  + [jax.ShapeDtypeStruct((batch, vt_rows, seq), bf)],
        compiler_params=pltpu.CompilerParams(
            dimension_semantics=("arbitrary", "arbitrary"), vmem_limit_bytes=VMEM_LIMIT),
        name="proj",
    )(x2, g1, win, winvt, gq, wq, gkv, wkvk, wkvvt, ggq, ggk, *tables)


def _attn_kernel(qm_ref, qg_ref, km_ref, kg_ref, vt_ref, x_ref, gnm_ref, gng_ref, wo_ref, h_ref,
                 st_ref, p_ref, ot_ref, o_ref):
    tq = qm_ref.shape[0]
    seq = km_ref.shape[0]
    kc = min(KEY_CHUNK, seq)
    n_chunks = seq // kc
    bf = jnp.bfloat16
    units = []
    for h in range(MLA_HEADS):
        units.append((km_ref, qm_ref, 2 * LANES * h, 2 * LANES, 2 * LANES * h, VT_ROWS * h))
    for h in range(GQA_HEADS):
        j = h // GQA_GROUP
        units.append((kg_ref, qg_ref, LANES * j, LANES, LANES * h, VT_ROWS * (MLA_HEADS + j)))
    n_units = len(units)
    col_max = [None] * n_units

    def scores(u, c):
        k_ref, q_ref, k_lo, w, q_lo, _ = units[u]
        s = _dot_t(k_ref[kc * c:kc * (c + 1), k_lo:k_lo + w], q_ref[:, q_lo:q_lo + w])
        st_ref[u % 2, kc * c:kc * (c + 1), :] = s
        cm = jnp.max(s.reshape(kc // 8, 8, tq), axis=0)
        col_max[u] = cm if c == 0 else jnp.maximum(col_max[u], cm)
        if c == n_chunks - 1:
            col_max[u] = jnp.broadcast_to(jnp.max(col_max[u], axis=0, keepdims=True), (8, tq))

    def probs(u, c):
        s = st_ref[u % 2, kc * c:kc * (c + 1), :].reshape(kc // 8, 8, tq)
        p = jnp.exp2(s - col_max[u][None]).reshape(kc, tq)
        p_ref[u % 2, kc * c:kc * (c + 1), :] = p.astype(bf)

    def weighted_values(u, c):
        vt_lo = units[u][5]
        pv = _dot(vt_ref[vt_lo:vt_lo + VT_ROWS, kc * c:kc * (c + 1)],
                  p_ref[u % 2, kc * c:kc * (c + 1), :])
        if c == 0:
            ot_ref[u % 2] = pv
        else:
            ot_ref[u % 2] += pv
        if c == n_chunks - 1:
            o_t = ot_ref[u % 2, :MLA_V, :] / ot_ref[u % 2, MLA_V:MLA_V + 1, :]
            o_ref[:, LANES * u:LANES * (u + 1)] = o_t.T

    for t in range(n_units + 2):
        for c in range(n_chunks):
            if t < n_units:
                scores(t, c)
            if 0 <= t - 1 < n_units:
                probs(t - 1, c)
            if 0 <= t - 2 < n_units:
                weighted_values(t - 2, c)

    mla_w = MLA_HEADS * MLA_V
    o = jnp.concatenate([_rms(o_ref[:, :mla_w], gnm_ref[...]),
                         _rms(o_ref[:, mla_w:], gng_ref[...])], axis=-1)
    h_ref[...] = x_ref[...] + _dot(o.astype(bf), wo_ref[...])


def _attention(qm, qg, km, kg, vt, x2, gnm, gng, wo, batch, seq, tq):
    tokens, d_model = x2.shape
    nq = seq // tq
    row = lambda b, qi: (b * nq + qi, 0)
    per_batch = lambda b, qi: (b, 0)
    const = lambda b, qi: (0, 0)

    def full(a):
        return pl.BlockSpec(a.shape, const)

    return pl.pallas_call(
        _attn_kernel,
        grid=(batch, nq),
        in_specs=[pl.BlockSpec((tq, qm.shape[1]), row), pl.BlockSpec((tq, qg.shape[1]), row),
                  pl.BlockSpec((seq, km.shape[1]), per_batch),
                  pl.BlockSpec((seq, kg.shape[1]), per_batch),
                  pl.BlockSpec((None, vt.shape[1], seq), lambda b, qi: (b, 0, 0)),
                  pl.BlockSpec((tq, d_model), row), full(gnm), full(gng), full(wo)],
        out_specs=pl.BlockSpec((tq, d_model), row),
        out_shape=jax.ShapeDtypeStruct((tokens, d_model), jnp.float32),
        scratch_shapes=[pltpu.VMEM((2, seq, tq), jnp.float32),
                        pltpu.VMEM((2, seq, tq), jnp.bfloat16),
                        pltpu.VMEM((2, VT_ROWS, tq), jnp.float32),
                        pltpu.VMEM((tq, d_model), jnp.float32)],
        compiler_params=pltpu.CompilerParams(
            dimension_semantics=("arbitrary", "arbitrary"), vmem_limit_bytes=VMEM_LIMIT),
        name="attn",
    )(qm, qg, km, kg, vt, x2, gnm, gng, wo)


def _ffn_kernel(h_ref, hp_ref, hn_ref, g2_ref, wup_ref, cw_ref, cb_ref, wdn_ref, gf_ref,
                o_ref, u_ref, hu_ref, act_ref, *, tiles_per_seq, final_norm):
    tm = h_ref.shape[0]
    bf = jnp.bfloat16
    i = pl.program_id(0)
    si = i % tiles_per_seq
    keep_prev = (si != 0).astype(jnp.float32)
    keep_next = (si != tiles_per_seq - 1).astype(jnp.float32)
    g2 = g2_ref[...]
    u_ref[0:HALO, :] = (_rms(hp_ref[...], g2) * keep_prev).astype(bf)
    u_ref[HALO:HALO + tm, :] = _rms(h_ref[...], g2).astype(bf)
    u_ref[HALO + tm:, :] = (_rms(hn_ref[...], g2) * keep_next).astype(bf)
    lane_tiles = FF_CHUNK // LANES

    def col(part, c, j=0):
        return part * D_FF + FF_CHUNK * c + LANES * j

    def up(c):
        u = u_ref[...]
        for part in range(2):
            t = _dot(u, wup_ref[:, col(part, c):col(part, c) + FF_CHUNK])
            for j in range(lane_tiles):
                hu_ref[c % 2, part, j] = t[:, LANES * j:LANES * (j + 1)]

    def conv(c, part, j):
        lo = col(part, c, j)
        w = cw_ref[:, lo:lo + LANES]
        src = hu_ref.at[c % 2, part, j]
        return (src[pl.ds(HALO - 1, tm), :] * w[0:1] + src[pl.ds(HALO, tm), :] * w[1:2]
                + src[pl.ds(HALO + 1, tm), :] * w[2:3] + cb_ref[:, lo:lo + LANES])

    def act(c):
        for j in range(lane_tiles):
            gate = conv(c, 0, j)
            lo = col(0, c, j)
            act_ref[:, lo:lo + LANES] = (gate * jax.nn.sigmoid(gate) * conv(c, 1, j)).astype(bf)

    up(0)
    for c in range(N_FF_CHUNKS):
        if c + 1 < N_FF_CHUNKS:
            up(c + 1)
        act(c)
    out = h_ref[...] + _dot(act_ref[...], wdn_ref[...])
    if final_norm:
        out = _rms(out, gf_ref[...])
    o_ref[...] = out


def _conv_ffn(h2, g2, wup, cw, cb, wdn, gf, seq, tm, final_norm):
    tokens, d_model = h2.shape
    tiles_per_seq = seq // tm
    n_halo_blocks = tokens // HALO
    per_tile = tm // HALO
    const = lambda i: (0, 0)
    kern = functools.partial(_ffn_kernel, tiles_per_seq=tiles_per_seq, final_norm=final_norm)
    return pl.pallas_call(
        kern,
        grid=(tokens // tm,),
        in_specs=[pl.BlockSpec((tm, d_model), lambda i: (i, 0)),
                  pl.BlockSpec((HALO, d_model), lambda i: (jnp.maximum(i * per_tile - 1, 0), 0)),
                  pl.BlockSpec((HALO, d_model),
                               lambda i: (jnp.minimum((i + 1) * per_tile, n_halo_blocks - 1), 0)),
                  pl.BlockSpec(g2.shape, const),
                  pl.BlockSpec(wup.shape, const, pipeline_mode=pl.Buffered(1)),
                  pl.BlockSpec(cw.shape, const), pl.BlockSpec(cb.shape, const),
                  pl.BlockSpec(wdn.shape, const, pipeline_mode=pl.Buffered(1)),
                  pl.BlockSpec(gf.shape, const)],
        out_specs=pl.BlockSpec((tm, d_model), lambda i: (i, 0)),
        out_shape=jax.ShapeDtypeStruct((tokens, d_model), jnp.float32),
        scratch_shapes=[pltpu.VMEM((tm + 2 * HALO, d_model), jnp.bfloat16),
                        pltpu.VMEM((2, 2, FF_CHUNK // LANES, tm + 2 * HALO, LANES), jnp.float32),
                        pltpu.VMEM((tm, D_FF), jnp.bfloat16)],
        compiler_params=pltpu.CompilerParams(
            dimension_semantics=("arbitrary",), vmem_limit_bytes=VMEM_LIMIT),
        name="convffn",
    )(h2, h2, h2, g2, wup, cw, cb, wdn, gf)


def _rope_np(pos, dim):
    inv = np.float32(ROPE_THETA) ** (-np.arange(0, dim, 2, dtype=np.float32) / np.float32(dim))
    ang = pos.astype(np.float32)[:, None] * inv[None, :]
    return np.cos(ang), np.sin(ang)


def _lane_tables(cos_a, sin_a, cos_b, sin_b):
    c = np.concatenate([cos_a, cos_b, cos_a, cos_b], axis=-1)
    s = np.concatenate([-sin_a, -sin_b, sin_a, sin_b], axis=-1)
    return jnp.asarray(c), jnp.asarray(s)


def _pair_perm(base_a, base_b, width):
    half = width // 2
    x1, x2 = np.arange(half), np.arange(half, width)
    return np.concatenate([base_a + x1, base_b + x1, base_a + x2, base_b + x2])


def _pick_tile(seq, want):
    t = min(seq, want)
    assert seq % t == 0 and t % HALO == 0
    return t


def kernel(x, norm1_g, w_in, mla_q_norm_g, mla_w_q_up, mla_kv_norm_g, mla_w_kv_up, gqa_q_norm_g,
           gqa_k_norm_g, group_norm_mla_g, group_norm_gqa_g, w_out, norm2_g, ffn_w_up, ffn_conv_w,
           ffn_conv_b, ffn_w_down, final_norm_g):
    batch, seq, d_model = x.shape
    depth = w_in.shape[0]
    bf = jnp.bfloat16
    tm_proj = _pick_tile(seq, 512)
    tq = _pick_tile(seq, 512)
    tm_ffn = _pick_tile(seq, 512)

    t = np.arange(seq)
    cos1, sin1 = _rope_np(t, MLA_ROPE)
    cos_r, sin_r = _rope_np(t // GRID_W, GQA_HEAD_DIM // 2)
    cos_c, sin_c = _rope_np(t % GRID_W, GQA_HEAD_DIM // 2)
    tables = _lane_tables(cos1, sin1, cos1, sin1) + _lane_tables(cos_r, sin_r, cos_c, sin_c)

    gqa_perm = _pair_perm(0, GQA_HEAD_DIM // 2, GQA_HEAD_DIM // 2)
    in_cols = np.concatenate(
        [np.arange(OFF_KPE), _pair_perm(OFF_KPE, OFF_KPE, MLA_ROPE)]
        + [OFF_GQ + GQA_HEAD_DIM * h + gqa_perm for h in range(GQA_HEADS)]
        + [OFF_GK + GQA_HEAD_DIM * j + gqa_perm for j in range(GQA_KV_HEADS)])
    q_cols = np.concatenate(
        [MLA_QK * h + np.arange(MLA_NOPE) for h in range(MLA_HEADS)]
        + [_pair_perm(MLA_QK * a + MLA_NOPE, MLA_QK * (a + 1) + MLA_NOPE, MLA_ROPE)
           for a in range(0, MLA_HEADS, 2)])
    kv_width = MLA_NOPE + MLA_V
    k_cols = np.concatenate([kv_width * h + np.arange(MLA_NOPE) for h in range(MLA_HEADS)])
    v_cols = np.concatenate([kv_width * h + MLA_NOPE + np.arange(MLA_V) for h in range(MLA_HEADS)])

    row = lambda v: v.reshape(1, -1)
    h2 = x.reshape(batch * seq, d_model)
    for l in range(depth):
        win = jnp.take(w_in[l], in_cols, axis=1).astype(bf)
        winvt = w_in[l][:, OFF_GV:IN_WIDTH].T.astype(bf)
        wq = jnp.take(mla_w_q_up[l], q_cols, axis=1).astype(bf)
        wkvk = jnp.take(mla_w_kv_up[l], k_cols, axis=1).astype(bf)
        wkvvt = jnp.take(mla_w_kv_up[l], v_cols, axis=1).T.astype(bf)
        qm, km, qg, kg, vt = _projections(
            h2, row(norm1_g[l]), win, winvt, row(mla_q_norm_g[l]), wq, row(mla_kv_norm_g[l]),
            wkvk, wkvvt, row(gqa_q_norm_g[l][gqa_perm]), row(gqa_k_norm_g[l][gqa_perm]), tables,
            batch, seq, tm_proj)
        h2 = _attention(qm, qg, km, kg, vt, h2, row(group_norm_mla_g[l]),
                        row(group_norm_gqa_g[l]), w_out[l].astype(bf), batch, seq, tq)
        h2 = _conv_ffn(h2, row(norm2_g[l]), ffn_w_up[l].astype(bf), ffn_conv_w[l],
                       row(ffn_conv_b[l]), ffn_w_down[l].astype(bf), row(final_norm_g), seq, tm_ffn,
                       final_norm=(l == depth - 1))
    return h2.reshape(batch, seq, d_model)
```

```python
import functools

import numpy as np

import jax
import jax.numpy as jnp
from jax import lax
from jax.experimental import pallas as pl
from jax.experimental.pallas import tpu as pltpu

EPS = 1e-6
LOG2_E = 1.4426950408889634
ROPE_THETA = 10000.0
GRID_W = 64

D_MODEL = 1024
MLA_HEADS = 4
MLA_Q_RANK = 384
MLA_KV_RANK = 256
MLA_NOPE = 128
MLA_ROPE = 64
MLA_V = 128
MLA_QK = MLA_NOPE + MLA_ROPE

GQA_HEADS = 4
GQA_KV_HEADS = 2
GQA_HEAD_DIM = 128
GQA_GROUP = GQA_HEADS // GQA_KV_HEADS

D_FF = 2816
LANES = 128
BF16_ROWS = 16

OFF_KPE = MLA_Q_RANK + MLA_KV_RANK
OFF_GQ = OFF_KPE + MLA_ROPE
OFF_GK = OFF_GQ + GQA_HEADS * GQA_HEAD_DIM
OFF_GV = OFF_GK + GQA_KV_HEADS * GQA_HEAD_DIM
IN_WIDTH = OFF_GV + GQA_KV_HEADS * GQA_HEAD_DIM

Z_CQ = 0
Z_CKV = Z_CQ + MLA_Q_RANK
Z_KPE = Z_CKV + MLA_KV_RANK
Z_GQ = Z_KPE + LANES
Z_GK = Z_GQ + GQA_HEADS * GQA_HEAD_DIM
Z_WIDTH = Z_GK + GQA_KV_HEADS * GQA_HEAD_DIM

N_V_HEADS = MLA_HEADS + GQA_KV_HEADS
VT_ROWS = MLA_V + BF16_ROWS

FF_CHUNK = 256
N_FF_CHUNKS = D_FF // FF_CHUNK
HALO = 16
KEY_CHUNK = 512
PV_DOTS = 4
PV_LAG = 2
PROJ_SPLIT = 2

VMEM_LIMIT = 56 * 1024 * 1024


def _rms(x, g):
    y = x * lax.rsqrt(jnp.mean(x * x, axis=-1, keepdims=True) + EPS)
    return y * g


def _rope(x, c, s):
    return x * c + pltpu.roll(x, LANES // 2, 1) * s


def _dot(a, b):
    return jnp.dot(a, b, preferred_element_type=jnp.float32)


def _dot_t(a, b):
    return lax.dot_general(a, b, (((1,), (1,)), ((), ())), preferred_element_type=jnp.float32)


def _proj_kernel(x_ref, g1_ref, win_ref, winvt_ref, gq_ref, wq_ref, gkv_ref, wkvk_ref, wkvvt_ref,
                 ggq_ref, ggk_ref, c1_ref, s1_ref, cg_ref, sg_ref,
                 qm_ref, km_ref, qg_ref, kg_ref, vt_ref):
    bf = jnp.bfloat16
    tm = x_ref.shape[0]
    rows_per = tm // PROJ_SPLIT
    scale_m = MLA_QK ** -0.5 * LOG2_E
    scale_g = GQA_HEAD_DIM ** -0.5 * LOG2_E
    nope_w = MLA_HEADS * MLA_NOPE

    for r in range(PROJ_SPLIT):
        rows = pl.ds(r * rows_per, rows_per)
        cols = slice(r * rows_per, (r + 1) * rows_per)
        u = _rms(x_ref[rows, :], g1_ref[...]).astype(bf)
        z = _dot(u, win_ref[...])
        c_q = _rms(z[:, Z_CQ:Z_CKV], gq_ref[...]).astype(bf)
        q = _dot(c_q, wq_ref[...])
        c_kv = _rms(z[:, Z_CKV:Z_KPE], gkv_ref[...]).astype(bf)
        k_nope = _dot(c_kv, wkvk_ref[...])
        vt_mla = _dot_t(wkvvt_ref[...], c_kv)
        vt_gqa = _dot_t(winvt_ref[...], u)

        c1, s1 = c1_ref[rows, :], s1_ref[rows, :]
        qm_ref[rows, :nope_w] = (q[:, :nope_w] * scale_m).astype(bf)
        for i in range(MLA_HEADS // 2):
            lo = nope_w + LANES * i
            qm_ref[rows, lo:lo + LANES] = (_rope(q[:, lo:lo + LANES], c1, s1) * scale_m).astype(bf)
        km_ref[rows, :nope_w] = k_nope.astype(bf)
        km_ref[rows, nope_w:] = _rope(z[:, Z_KPE:Z_GQ], c1, s1).astype(bf)
        for h in range(MLA_HEADS):
            vt_ref[VT_ROWS * h:VT_ROWS * h + MLA_V, cols] = vt_mla[MLA_V * h:MLA_V * (h + 1)].astype(bf)

        cg, sg = cg_ref[rows, :], sg_ref[rows, :]
        for h in range(GQA_HEADS):
            t = _rms(z[:, Z_GQ + LANES * h:Z_GQ + LANES * (h + 1)], ggq_ref[...])
            qg_ref[rows, LANES * h:LANES * (h + 1)] = (_rope(t, cg, sg) * scale_g).astype(bf)
        for j in range(GQA_KV_HEADS):
            t = _rms(z[:, Z_GK + LANES * j:Z_GK + LANES * (j + 1)], ggk_ref[...])
            kg_ref[rows, LANES * j:LANES * (j + 1)] = _rope(t, cg, sg).astype(bf)
            lo = VT_ROWS * (MLA_HEADS + j)
            vt_ref[lo:lo + MLA_V, cols] = vt_gqa[LANES * j:LANES * (j + 1)].astype(bf)

    ones = jnp.ones((BF16_ROWS, tm), bf)
    for h in range(N_V_HEADS):
        vt_ref[VT_ROWS * h + MLA_V:VT_ROWS * (h + 1), :] = ones


def _projections(x2, g1, win, winvt, gq, wq, gkv, wkvk, wkvvt, ggq, ggk, tables, batch, seq, tm):
    tokens, d_model = x2.shape
    ns = seq // tm
    row = lambda si, b: (b * ns + si, 0)
    const = lambda si, b: (0, 0)
    tab = lambda si, b: (si, 0)

    def full(a):
        return pl.BlockSpec(a.shape, const)

    bf = jnp.bfloat16
    widths = (MLA_HEADS * MLA_NOPE + LANES * (MLA_HEADS // 2), MLA_HEADS * MLA_NOPE + LANES,
              GQA_HEADS * GQA_HEAD_DIM, GQA_KV_HEADS * GQA_HEAD_DIM)
    vt_rows = N_V_HEADS * VT_ROWS
    return pl.pallas_call(
        _proj_kernel,
        grid=(ns, batch),
        in_specs=[pl.BlockSpec((tm, d_model), row), full(g1), full(win), full(winvt), full(gq),
                  full(wq), full(gkv), full(wkvk), full(wkvvt), full(ggq), full(ggk)]
                 + [pl.BlockSpec((tm, LANES), tab) for _ in tables],
        out_specs=[pl.BlockSpec((tm, w), row) for w in widths]
                  + [pl.BlockSpec((None, vt_rows, tm), lambda si, b: (b, 0, si))],
        out_shape=[jax.ShapeDtypeStruct((tokens, w), bf) for w in widths]
                  + [jax.ShapeDtypeStruct((batch, vt_rows, seq), bf)],
        compiler_params=pltpu.CompilerParams(
            dimension_semantics=("arbitrary", "arbitrary"), vmem_limit_bytes=VMEM_LIMIT),
        name="proj",
    )(x2, g1, win, winvt, gq, wq, gkv, wkvk, wkvvt, ggq, ggk, *tables)


def _attn_kernel(qm_ref, qg_ref, km_ref, kg_ref, vt_ref, x_ref, gnm_ref, gng_ref, wo_ref, h_ref,
                 st_ref, p_ref, ot_ref, o_ref):
    tq = qm_ref.shape[0]
    seq = km_ref.shape[0]
    kc = min(KEY_CHUNK, seq)
    n_chunks = seq // kc
    bf = jnp.bfloat16
    nope_w = MLA_HEADS * MLA_NOPE
    lane = lax.broadcasted_iota(jnp.int32, (1, LANES), 1)
    first_of_pair = (lane % (LANES // 2)) < (MLA_ROPE // 2)

    def mla_operands(h, keys):
        own = first_of_pair if h % 2 == 0 else jnp.logical_not(first_of_pair)
        rope_lo = nope_w + LANES * (h // 2)
        q_rope = jnp.where(own, qm_ref[:, rope_lo:rope_lo + LANES], jnp.zeros((), bf))
        q = jnp.concatenate([qm_ref[:, LANES * h:LANES * (h + 1)], q_rope], axis=1)
        k = jnp.concatenate([km_ref[keys, LANES * h:LANES * (h + 1)],
                             km_ref[keys, nope_w:nope_w + LANES]], axis=1)
        return k, q

    def gqa_operands(h, keys):
        j = h // GQA_GROUP
        return kg_ref[keys, LANES * j:LANES * (j + 1)], qg_ref[:, LANES * h:LANES * (h + 1)]

    units = ([(mla_operands, h, VT_ROWS * h) for h in range(MLA_HEADS)]
             + [(gqa_operands, h, VT_ROWS * (MLA_HEADS + h // GQA_GROUP)) for h in range(GQA_HEADS)])
    n_units = len(units)
    col_max = [None] * n_units

    def scores(u, c):
        operands, h, _ = units[u]
        s = _dot_t(*operands(h, slice(kc * c, kc * (c + 1))))
        st_ref[u % 2, kc * c:kc * (c + 1), :] = s
        cm = jnp.max(s.reshape(kc // 8, 8, tq), axis=0)
        col_max[u] = cm if c == 0 else jnp.maximum(col_max[u], cm)
        if c == n_chunks - 1:
            col_max[u] = jnp.broadcast_to(jnp.max(col_max[u], axis=0, keepdims=True), (8, tq))

    def probs(u, c):
        s = st_ref[u % 2, kc * c:kc * (c + 1), :].reshape(kc // 8, 8, tq)
        p = jnp.exp2(s - col_max[u][None]).reshape(kc, tq)
        p_ref[u % 2, kc * c:kc * (c + 1), :] = p.astype(bf)

    pv_every = max(n_chunks // PV_DOTS, 1)

    def weighted_values(u, c):
        if (c + 1) % pv_every:
            return
        vt_lo = units[u][2]
        keys = slice(kc * (c + 1 - pv_every), kc * (c + 1))
        pv = _dot(vt_ref[vt_lo:vt_lo + VT_ROWS, keys], p_ref[u % 2, keys, :])
        if c + 1 != pv_every:
            pv += ot_ref[u % 2]
        if c != n_chunks - 1:
            ot_ref[u % 2] = pv
        else:
            o_ref[:, LANES * u:LANES * (u + 1)] = (pv[:MLA_V] / pv[MLA_V:MLA_V + 1]).T

    for t in range(n_units + PV_LAG):
        for c in range(n_chunks):
            if t < n_units:
                scores(t, c)
            if 0 <= t - 1 < n_units:
                probs(t - 1, c)
            if 0 <= t - PV_LAG < n_units:
                weighted_values(t - PV_LAG, c)

    mla_w = MLA_HEADS * MLA_V
    o = jnp.concatenate([_rms(o_ref[:, :mla_w], gnm_ref[...]),
                         _rms(o_ref[:, mla_w:], gng_ref[...])], axis=-1)
    h_ref[...] = x_ref[...] + _dot(o.astype(bf), wo_ref[...])


def _attention(qm, qg, km, kg, vt, x2, gnm, gng, wo, batch, seq, tq):
    tokens, d_model = x2.shape
    nq = seq // tq
    row = lambda b, qi: (b * nq + qi, 0)
    per_batch = lambda b, qi: (b, 0)
    const = lambda b, qi: (0, 0)

    def full(a):
        return pl.BlockSpec(a.shape, const)

    return pl.pallas_call(
        _attn_kernel,
        grid=(batch, nq),
        in_specs=[pl.BlockSpec((tq, qm.shape[1]), row), pl.BlockSpec((tq, qg.shape[1]), row),
                  pl.BlockSpec((seq, km.shape[1]), per_batch),
                  pl.BlockSpec((seq, kg.shape[1]), per_batch),
                  pl.BlockSpec((None, vt.shape[1], seq), lambda b, qi: (b, 0, 0)),
                  pl.BlockSpec((tq, d_model), row), full(gnm), full(gng), full(wo)],
        out_specs=pl.BlockSpec((tq, d_model), row),
        out_shape=jax.ShapeDtypeStruct((tokens, d_model), jnp.float32),
        scratch_shapes=[pltpu.VMEM((2, seq, tq), jnp.float32),
                        pltpu.VMEM((2, seq, tq), jnp.bfloat16),
                        pltpu.VMEM((2, VT_ROWS, tq), jnp.float32),
                        pltpu.VMEM((tq, d_model), jnp.float32)],
        compiler_params=pltpu.CompilerParams(
            dimension_semantics=("arbitrary", "arbitrary"), vmem_limit_bytes=VMEM_LIMIT),
        name="attn",
    )(qm, qg, km, kg, vt, x2, gnm, gng, wo)


def _ffn_kernel(h_ref, hp_ref, hn_ref, g2_ref, wup_ref, cw_ref, cb_ref, wdn_ref, gf_ref,
                o_ref, u_ref, hu_ref, act_ref, *, tiles_per_seq, final_norm):
    tm = h_ref.shape[0]
    bf = jnp.bfloat16
    i = pl.program_id(0)
    si = i % tiles_per_seq
    keep_prev = (si != 0).astype(jnp.float32)
    keep_next = (si != tiles_per_seq - 1).astype(jnp.float32)
    g2 = g2_ref[...]
    u_ref[0:HALO, :] = (_rms(hp_ref[...], g2) * keep_prev).astype(bf)
    u_ref[HALO:HALO + tm, :] = _rms(h_ref[...], g2).astype(bf)
    u_ref[HALO + tm:, :] = (_rms(hn_ref[...], g2) * keep_next).astype(bf)
    lane_tiles = FF_CHUNK // LANES

    def col(part, c, j=0):
        return part * D_FF + FF_CHUNK * c + LANES * j

    def up(c, part):
        t = _dot(u_ref[...], wup_ref[:, col(part, c):col(part, c) + FF_CHUNK])
        for j in range(lane_tiles):
            hu_ref[c % 2, part, j] = t[:, LANES * j:LANES * (j + 1)]

    def conv(c, part, j):
        lo = col(part, c, j)
        w = cw_ref[:, lo:lo + LANES]
        src = hu_ref.at[c % 2, part, j]
        return (src[pl.ds(HALO - 1, tm), :] * w[0:1] + src[pl.ds(HALO, tm), :] * w[1:2]
                + src[pl.ds(HALO + 1, tm), :] * w[2:3] + cb_ref[:, lo:lo + LANES])

    def act(c, j):
        gate = conv(c, 0, j)
        lo = col(0, c, j)
        act_ref[:, lo:lo + LANES] = (gate * jax.nn.sigmoid(gate) * conv(c, 1, j)).astype(bf)

    assert lane_tiles == 2
    up(0, 0)
    up(0, 1)
    for c in range(N_FF_CHUNKS):
        for half in range(2):
            if c + 1 < N_FF_CHUNKS:
                up(c + 1, half)
            act(c, half)
    out = h_ref[...] + _dot(act_ref[...], wdn_ref[...])
    if final_norm:
        out = _rms(out, gf_ref[...])
    o_ref[...] = out


def _conv_ffn(h2, g2, wup, cw, cb, wdn, gf, seq, tm, final_norm):
    tokens, d_model = h2.shape
    tiles_per_seq = seq // tm
    n_halo_blocks = tokens // HALO
    per_tile = tm // HALO
    const = lambda i: (0, 0)
    kern = functools.partial(_ffn_kernel, tiles_per_seq=tiles_per_seq, final_norm=final_norm)
    return pl.pallas_call(
        kern,
        grid=(tokens // tm,),
        in_specs=[pl.BlockSpec((tm, d_model), lambda i: (i, 0)),
                  pl.BlockSpec((HALO, d_model), lambda i: (jnp.maximum(i * per_tile - 1, 0), 0)),
                  pl.BlockSpec((HALO, d_model),
                               lambda i: (jnp.minimum((i + 1) * per_tile, n_halo_blocks - 1), 0)),
                  pl.BlockSpec(g2.shape, const),
                  pl.BlockSpec(wup.shape, const, pipeline_mode=pl.Buffered(1)),
                  pl.BlockSpec(cw.shape, const), pl.BlockSpec(cb.shape, const),
                  pl.BlockSpec(wdn.shape, const, pipeline_mode=pl.Buffered(1)),
                  pl.BlockSpec(gf.shape, const)],
        out_specs=pl.BlockSpec((tm, d_model), lambda i: (i, 0)),
        out_shape=jax.ShapeDtypeStruct((tokens, d_model), jnp.float32),
        scratch_shapes=[pltpu.VMEM((tm + 2 * HALO, d_model), jnp.bfloat16),
                        pltpu.VMEM((2, 2, FF_CHUNK // LANES, tm + 2 * HALO, LANES), jnp.float32),
                        pltpu.VMEM((tm, D_FF), jnp.bfloat16)],
        compiler_params=pltpu.CompilerParams(
            dimension_semantics=("arbitrary",), vmem_limit_bytes=VMEM_LIMIT),
        name="convffn",
    )(h2, h2, h2, g2, wup, cw, cb, wdn, gf)


def _rope_np(pos, dim):
    inv = np.float32(ROPE_THETA) ** (-np.arange(0, dim, 2, dtype=np.float32) / np.float32(dim))
    ang = pos.astype(np.float32)[:, None] * inv[None, :]
    return np.cos(ang), np.sin(ang)


def _lane_tables(cos_a, sin_a, cos_b, sin_b):
    c = np.concatenate([cos_a, cos_b, cos_a, cos_b], axis=-1)
    s = np.concatenate([-sin_a, -sin_b, sin_a, sin_b], axis=-1)
    return jnp.asarray(c), jnp.asarray(s)


def _interleave_pairs(w, width):
    lead = w.shape[:-1]
    w = w.reshape(lead + (-1, 2, 2, width // 2))
    return jnp.swapaxes(w, -3, -2).reshape(lead + (-1,))


def _pick_tile(seq, want):
    t = min(seq, want)
    assert seq % t == 0 and t % HALO == 0
    return t


def kernel(x, norm1_g, w_in, mla_q_norm_g, mla_w_q_up, mla_kv_norm_g, mla_w_kv_up, gqa_q_norm_g,
           gqa_k_norm_g, group_norm_mla_g, group_norm_gqa_g, w_out, norm2_g, ffn_w_up, ffn_conv_w,
           ffn_conv_b, ffn_w_down, final_norm_g):
    batch, seq, d_model = x.shape
    depth = w_in.shape[0]
    bf = jnp.bfloat16
    tm_proj = _pick_tile(seq, 512)
    tq = _pick_tile(seq, 512)
    tm_ffn = _pick_tile(seq, 512)

    t = np.arange(seq)
    cos1, sin1 = _rope_np(t, MLA_ROPE)
    cos_r, sin_r = _rope_np(t // GRID_W, GQA_HEAD_DIM // 2)
    cos_c, sin_c = _rope_np(t % GRID_W, GQA_HEAD_DIM // 2)
    tables = _lane_tables(cos1, sin1, cos1, sin1) + _lane_tables(cos_r, sin_r, cos_c, sin_c)

    row = lambda v: v.reshape(1, -1)
    h2 = x.reshape(batch * seq, d_model)
    for l in range(depth):
        kpe_w = w_in[l][:, OFF_KPE:OFF_GQ]
        win = jnp.concatenate(
            [w_in[l][:, :OFF_KPE],
             _interleave_pairs(jnp.concatenate([kpe_w, kpe_w], axis=1), MLA_ROPE),
             _interleave_pairs(w_in[l][:, OFF_GQ:OFF_GV], GQA_HEAD_DIM // 2)],
            axis=1).astype(bf)
        winvt = w_in[l][:, OFF_GV:IN_WIDTH].T.astype(bf)
        wq3 = mla_w_q_up[l].reshape(MLA_Q_RANK, MLA_HEADS, MLA_QK)
        wq = jnp.concatenate(
            [wq3[:, :, :MLA_NOPE].reshape(MLA_Q_RANK, -1),
             _interleave_pairs(wq3[:, :, MLA_NOPE:].reshape(MLA_Q_RANK, -1), MLA_ROPE)],
            axis=1).astype(bf)
        wkv3 = mla_w_kv_up[l].reshape(MLA_KV_RANK, MLA_HEADS, MLA_NOPE + MLA_V)
        wkvk = wkv3[:, :, :MLA_NOPE].reshape(MLA_KV_RANK, -1).astype(bf)
        wkvvt = wkv3[:, :, MLA_NOPE:].reshape(MLA_KV_RANK, -1).T.astype(bf)
        qm, km, qg, kg, vt = _projections(
            h2, row(norm1_g[l]), win, winvt, row(mla_q_norm_g[l]), wq, row(mla_kv_norm_g[l]),
            wkvk, wkvvt, row(_interleave_pairs(gqa_q_norm_g[l], GQA_HEAD_DIM // 2)),
            row(_interleave_pairs(gqa_k_norm_g[l], GQA_HEAD_DIM // 2)), tables, batch, seq, tm_proj)
        h2 = _attention(qm, qg, km, kg, vt, h2, row(group_norm_mla_g[l]),
                        row(group_norm_gqa_g[l]), w_out[l].astype(bf), batch, seq, tq)
        h2 = _conv_ffn(h2, row(norm2_g[l]), ffn_w_up[l].astype(bf), ffn_conv_w[l],
                       row(ffn_conv_b[l]), ffn_w_down[l].astype(bf), row(final_norm_g), seq, tm_ffn,
                       final_norm=(l == depth - 1))
    return h2.reshape(batch, seq, d_model)
```

```python
import functools

import numpy as np

import jax
import jax.numpy as jnp
from jax import lax
from jax.experimental import pallas as pl
from jax.experimental.pallas import tpu as pltpu

EPS = 1e-6
LOG2_E = 1.4426950408889634
ROPE_THETA = 10000.0
GRID_W = 64

D_MODEL = 1024
MLA_HEADS = 4
MLA_Q_RANK = 384
MLA_KV_RANK = 256
MLA_NOPE = 128
MLA_ROPE = 64
MLA_V = 128
MLA_QK = MLA_NOPE + MLA_ROPE

GQA_HEADS = 4
GQA_KV_HEADS = 2
GQA_HEAD_DIM = 128
GQA_GROUP = GQA_HEADS // GQA_KV_HEADS

D_FF = 2816
LANES = 128
BF16_ROWS = 16

OFF_KPE = MLA_Q_RANK + MLA_KV_RANK
OFF_GQ = OFF_KPE + MLA_ROPE
OFF_GK = OFF_GQ + GQA_HEADS * GQA_HEAD_DIM
OFF_GV = OFF_GK + GQA_KV_HEADS * GQA_HEAD_DIM
IN_WIDTH = OFF_GV + GQA_KV_HEADS * GQA_HEAD_DIM

Z_CQ = 0
Z_CKV = Z_CQ + MLA_Q_RANK
Z_KPE = Z_CKV + MLA_KV_RANK
Z_GQ = Z_KPE + LANES
Z_GK = Z_GQ + GQA_HEADS * GQA_HEAD_DIM
Z_WIDTH = Z_GK + GQA_KV_HEADS * GQA_HEAD_DIM

N_V_HEADS = MLA_HEADS + GQA_KV_HEADS
VT_ROWS = MLA_V + BF16_ROWS

FF_CHUNK = 256
N_FF_CHUNKS = D_FF // FF_CHUNK
HALO = 16
KEY_CHUNK = 1024
PV_DOTS = 4
PV_LAG = 2
PROJ_SPLIT = 4

VMEM_LIMIT = 56 * 1024 * 1024


def _rms(x, g):
    y = x * lax.rsqrt(jnp.mean(x * x, axis=-1, keepdims=True) + EPS)
    return y * g


def _rope(x, c, s):
    return x * c + pltpu.roll(x, LANES // 2, 1) * s


def _dot(a, b):
    return jnp.dot(a, b, preferred_element_type=jnp.float32)


def _dot_t(a, b):
    return lax.dot_general(a, b, (((1,), (1,)), ((), ())), preferred_element_type=jnp.float32)


def _proj_kernel(x_ref, g1_ref, win_ref, winvt_ref, gq_ref, wq_ref, gkv_ref, wkvk_ref, wkvvt_ref,
                 ggq_ref, ggk_ref, c1_ref, s1_ref, cg_ref, sg_ref,
                 qm_ref, km_ref, qg_ref, kg_ref, vt_ref):
    bf = jnp.bfloat16
    tm = x_ref.shape[0]
    rows_per = tm // PROJ_SPLIT
    scale_m = MLA_QK ** -0.5 * LOG2_E
    scale_g = GQA_HEAD_DIM ** -0.5 * LOG2_E
    nope_w = MLA_HEADS * MLA_NOPE

    for r in range(PROJ_SPLIT):
        rows = pl.ds(r * rows_per, rows_per)
        cols = slice(r * rows_per, (r + 1) * rows_per)
        u = _rms(x_ref[rows, :], g1_ref[...]).astype(bf)
        z = _dot(u, win_ref[...])
        c_q = _rms(z[:, Z_CQ:Z_CKV], gq_ref[...]).astype(bf)
        q = _dot(c_q, wq_ref[...])
        c_kv = _rms(z[:, Z_CKV:Z_KPE], gkv_ref[...]).astype(bf)
        k_nope = _dot(c_kv, wkvk_ref[...])
        vt_mla = _dot_t(wkvvt_ref[...], c_kv)
        vt_gqa = _dot_t(winvt_ref[...], u)

        c1, s1 = c1_ref[rows, :], s1_ref[rows, :]
        qm_ref[rows, :nope_w] = (q[:, :nope_w] * scale_m).astype(bf)
        for i in range(MLA_HEADS // 2):
            lo = nope_w + LANES * i
            qm_ref[rows, lo:lo + LANES] = (_rope(q[:, lo:lo + LANES], c1, s1) * scale_m).astype(bf)
        km_ref[rows, :nope_w] = k_nope.astype(bf)
        km_ref[rows, nope_w:] = _rope(z[:, Z_KPE:Z_GQ], c1, s1).astype(bf)
        for h in range(MLA_HEADS):
            vt_ref[VT_ROWS * h:VT_ROWS * h + MLA_V, cols] = vt_mla[MLA_V * h:MLA_V * (h + 1)].astype(bf)

        cg, sg = cg_ref[rows, :], sg_ref[rows, :]
        for h in range(GQA_HEADS):
            t = _rms(z[:, Z_GQ + LANES * h:Z_GQ + LANES * (h + 1)], ggq_ref[...])
            qg_ref[rows, LANES * h:LANES * (h + 1)] = (_rope(t, cg, sg) * scale_g).astype(bf)
        for j in range(GQA_KV_HEADS):
            t = _rms(z[:, Z_GK + LANES * j:Z_GK + LANES * (j + 1)], ggk_ref[...])
            kg_ref[rows, LANES * j:LANES * (j + 1)] = _rope(t, cg, sg).astype(bf)
            lo = VT_ROWS * (MLA_HEADS + j)
            vt_ref[lo:lo + MLA_V, cols] = vt_gqa[LANES * j:LANES * (j + 1)].astype(bf)

    ones = jnp.ones((BF16_ROWS, tm), bf)
    for h in range(N_V_HEADS):
        vt_ref[VT_ROWS * h + MLA_V:VT_ROWS * (h + 1), :] = ones


def _projections(x2, g1, win, winvt, gq, wq, gkv, wkvk, wkvvt, ggq, ggk, tables, batch, seq, tm):
    tokens, d_model = x2.shape
    ns = seq // tm
    row = lambda si, b: (b * ns + si, 0)
    const = lambda si, b: (0, 0)
    tab = lambda si, b: (si, 0)

    def full(a):
        return pl.BlockSpec(a.shape, const)

    bf = jnp.bfloat16
    widths = (MLA_HEADS * MLA_NOPE + LANES * (MLA_HEADS // 2), MLA_HEADS * MLA_NOPE + LANES,
              GQA_HEADS * GQA_HEAD_DIM, GQA_KV_HEADS * GQA_HEAD_DIM)
    vt_rows = N_V_HEADS * VT_ROWS
    return pl.pallas_call(
        _proj_kernel,
        grid=(ns, batch),
        in_specs=[pl.BlockSpec((tm, d_model), row), full(g1), full(win), full(winvt), full(gq),
                  full(wq), full(gkv), full(wkvk), full(wkvvt), full(ggq), full(ggk)]
                 + [pl.BlockSpec((tm, LANES), tab) for _ in tables],
        out_specs=[pl.BlockSpec((tm, w), row) for w in widths]
                  + [pl.BlockSpec((None, vt_rows, tm), lambda si, b: (b, 0, si))],
        out_shape=[jax.ShapeDtypeStruct((tokens, w), bf) for w in widths]
                  + [jax.ShapeDtypeStruct((batch, vt_rows, seq), bf)],
        compiler_params=pltpu.CompilerParams(
            dimension_semantics=("arbitrary", "arbitrary"), vmem_limit_bytes=VMEM_LIMIT),
        name="proj",
    )(x2, g1, win, winvt, gq, wq, gkv, wkvk, wkvvt, ggq, ggk, *tables)


def _attn_kernel(qm_ref, qg_ref, km_ref, kg_ref, vt_ref, x_ref, gnm_ref, gng_ref, wo_ref, h_ref,
                 st_ref, p_ref, ot_ref, o_ref):
    tq = qm_ref.shape[0]
    seq = km_ref.shape[0]
    kc = min(KEY_CHUNK, seq)
    n_chunks = seq // kc
    bf = jnp.bfloat16
    nope_w = MLA_HEADS * MLA_NOPE
    lane = lax.broadcasted_iota(jnp.int32, (1, LANES), 1)
    first_of_pair = (lane % (LANES // 2)) < (MLA_ROPE // 2)

    def mla_operands(h, keys):
        own = first_of_pair if h % 2 == 0 else jnp.logical_not(first_of_pair)
        rope_lo = nope_w + LANES * (h // 2)
        q_rope = jnp.where(own, qm_ref[:, rope_lo:rope_lo + LANES], jnp.zeros((), bf))
        q = jnp.concatenate([qm_ref[:, LANES * h:LANES * (h + 1)], q_rope], axis=1)
        k = jnp.concatenate([km_ref[keys, LANES * h:LANES * (h + 1)],
                             km_ref[keys, nope_w:nope_w + LANES]], axis=1)
        return k, q

    def gqa_operands(h, keys):
        j = h // GQA_GROUP
        return kg_ref[keys, LANES * j:LANES * (j + 1)], qg_ref[:, LANES * h:LANES * (h + 1)]

    units = ([(mla_operands, h, VT_ROWS * h) for h in range(MLA_HEADS)]
             + [(gqa_operands, h, VT_ROWS * (MLA_HEADS + h // GQA_GROUP)) for h in range(GQA_HEADS)])
    n_units = len(units)
    col_max = [None] * n_units

    def scores(u, c):
        operands, h, _ = units[u]
        s = _dot_t(*operands(h, slice(kc * c, kc * (c + 1))))
        st_ref[u % 2, kc * c:kc * (c + 1), :] = s
        cm = jnp.max(s.reshape(kc // 8, 8, tq), axis=0)
        col_max[u] = cm if c == 0 else jnp.maximum(col_max[u], cm)
        if c == n_chunks - 1:
            col_max[u] = jnp.broadcast_to(jnp.max(col_max[u], axis=0, keepdims=True), (8, tq))

    def probs(u, c):
        s = st_ref[u % 2, kc * c:kc * (c + 1), :].reshape(kc // 8, 8, tq)
        p = jnp.exp2(s - col_max[u][None]).reshape(kc, tq)
        p_ref[u % 2, kc * c:kc * (c + 1), :] = p.astype(bf)

    pv_every = max(n_chunks // PV_DOTS, 1)

    def weighted_values(u, c):
        if (c + 1) % pv_every:
            return
        vt_lo = units[u][2]
        keys = slice(kc * (c + 1 - pv_every), kc * (c + 1))
        pv = _dot(vt_ref[vt_lo:vt_lo + VT_ROWS, keys], p_ref[u % 2, keys, :])
        if c + 1 != pv_every:
            pv += ot_ref[u % 2]
        if c != n_chunks - 1:
            ot_ref[u % 2] = pv
        else:
            o_ref[:, LANES * u:LANES * (u + 1)] = (pv[:MLA_V] / pv[MLA_V:MLA_V + 1]).T

    for t in range(n_units + PV_LAG):
        for c in range(n_chunks):
            if t < n_units:
                scores(t, c)
            if 0 <= t - 1 < n_units:
                probs(t - 1, c)
            if 0 <= t - PV_LAG < n_units:
                weighted_values(t - PV_LAG, c)

    mla_w = MLA_HEADS * MLA_V
    o = jnp.concatenate([_rms(o_ref[:, :mla_w], gnm_ref[...]),
                         _rms(o_ref[:, mla_w:], gng_ref[...])], axis=-1)
    h_ref[...] = x_ref[...] + _dot(o.astype(bf), wo_ref[...])


def _attention(qm, qg, km, kg, vt, x2, gnm, gng, wo, batch, seq, tq):
    tokens, d_model = x2.shape
    nq = seq // tq
    row = lambda b, qi: (b * nq + qi, 0)
    per_batch = lambda b, qi: (b, 0)
    const = lambda b, qi: (0, 0)

    def full(a):
        return pl.BlockSpec(a.shape, const)

    return pl.pallas_call(
        _attn_kernel,
        grid=(batch, nq),
        in_specs=[pl.BlockSpec((tq, qm.shape[1]), row), pl.BlockSpec((tq, qg.shape[1]), row),
                  pl.BlockSpec((seq, km.shape[1]), per_batch),
                  pl.BlockSpec((seq, kg.shape[1]), per_batch),
                  pl.BlockSpec((None, vt.shape[1], seq), lambda b, qi: (b, 0, 0)),
                  pl.BlockSpec((tq, d_model), row), full(gnm), full(gng), full(wo)],
        out_specs=pl.BlockSpec((tq, d_model), row),
        out_shape=jax.ShapeDtypeStruct((tokens, d_model), jnp.float32),
        scratch_shapes=[pltpu.VMEM((2, seq, tq), jnp.float32),
                        pltpu.VMEM((2, seq, tq), jnp.bfloat16),
                        pltpu.VMEM((2, VT_ROWS, tq), jnp.float32),
                        pltpu.VMEM((tq, d_model), jnp.float32)],
        compiler_params=pltpu.CompilerParams(
            dimension_semantics=("arbitrary", "arbitrary"), vmem_limit_bytes=VMEM_LIMIT),
        name="attn",
    )(qm, qg, km, kg, vt, x2, gnm, gng, wo)


def _ffn_kernel(h_ref, hp_ref, hn_ref, g2_ref, wup_ref, cw_ref, cb_ref, wdn_ref, gf_ref,
                o_ref, u_ref, hu_ref, act_ref, *, tiles_per_seq, final_norm):
    tm = h_ref.shape[0]
    bf = jnp.bfloat16
    i = pl.program_id(0)
    si = i % tiles_per_seq
    keep_prev = (si != 0).astype(jnp.float32)
    keep_next = (si != tiles_per_seq - 1).astype(jnp.float32)
    g2 = g2_ref[...]
    u_ref[0:HALO, :] = (_rms(hp_ref[...], g2) * keep_prev).astype(bf)
    u_ref[HALO:HALO + tm, :] = _rms(h_ref[...], g2).astype(bf)
    u_ref[HALO + tm:, :] = (_rms(hn_ref[...], g2) * keep_next).astype(bf)
    lane_tiles = FF_CHUNK // LANES

    def col(part, c, j=0):
        return part * D_FF + FF_CHUNK * c + LANES * j

    def up(c, part):
        t = _dot(u_ref[...], wup_ref[:, col(part, c):col(part, c) + FF_CHUNK])
        for j in range(lane_tiles):
            hu_ref[c % 2, part, j] = t[:, LANES * j:LANES * (j + 1)]

    def conv(c, part, j):
        lo = col(part, c, j)
        w = cw_ref[:, lo:lo + LANES]
        src = hu_ref.at[c % 2, part, j]
        return (src[pl.ds(HALO - 1, tm), :] * w[0:1] + src[pl.ds(HALO, tm), :] * w[1:2]
                + src[pl.ds(HALO + 1, tm), :] * w[2:3] + cb_ref[:, lo:lo + LANES])

    def act(c, j):
        gate = conv(c, 0, j)
        lo = col(0, c, j)
        act_ref[:, lo:lo + LANES] = (gate * jax.nn.sigmoid(gate) * conv(c, 1, j)).astype(bf)

    assert lane_tiles == 2
    up(0, 0)
    up(0, 1)
    for c in range(N_FF_CHUNKS):
        for half in range(2):
            if c + 1 < N_FF_CHUNKS:
                up(c + 1, half)
            act(c, half)
    out = h_ref[...] + _dot(act_ref[...], wdn_ref[...])
    if final_norm:
        out = _rms(out, gf_ref[...])
    o_ref[...] = out


def _conv_ffn(h2, g2, wup, cw, cb, wdn, gf, seq, tm, final_norm):
    tokens, d_model = h2.shape
    tiles_per_seq = seq // tm
    n_halo_blocks = tokens // HALO
    per_tile = tm // HALO
    const = lambda i: (0, 0)
    kern = functools.partial(_ffn_kernel, tiles_per_seq=tiles_per_seq, final_norm=final_norm)
    return pl.pallas_call(
        kern,
        grid=(tokens // tm,),
        in_specs=[pl.BlockSpec((tm, d_model), lambda i: (i, 0)),
                  pl.BlockSpec((HALO, d_model), lambda i: (jnp.maximum(i * per_tile - 1, 0), 0)),
                  pl.BlockSpec((HALO, d_model),
                               lambda i: (jnp.minimum((i + 1) * per_tile, n_halo_blocks - 1), 0)),
                  pl.BlockSpec(g2.shape, const),
                  pl.BlockSpec(wup.shape, const, pipeline_mode=pl.Buffered(1)),
                  pl.BlockSpec(cw.shape, const), pl.BlockSpec(cb.shape, const),
                  pl.BlockSpec(wdn.shape, const, pipeline_mode=pl.Buffered(1)),
                  pl.BlockSpec(gf.shape, const)],
        out_specs=pl.BlockSpec((tm, d_model), lambda i: (i, 0)),
        out_shape=jax.ShapeDtypeStruct((tokens, d_model), jnp.float32),
        scratch_shapes=[pltpu.VMEM((tm + 2 * HALO, d_model), jnp.bfloat16),
                        pltpu.VMEM((2, 2, FF_CHUNK // LANES, tm + 2 * HALO, LANES), jnp.float32),
                        pltpu.VMEM((tm, D_FF), jnp.bfloat16)],
        compiler_params=pltpu.CompilerParams(
            dimension_semantics=("arbitrary",), vmem_limit_bytes=VMEM_LIMIT),
        name="convffn",
    )(h2, h2, h2, g2, wup, cw, cb, wdn, gf)


def _rope_np(pos, dim):
    inv = np.float32(ROPE_THETA) ** (-np.arange(0, dim, 2, dtype=np.float32) / np.float32(dim))
    ang = pos.astype(np.float32)[:, None] * inv[None, :]
    return np.cos(ang), np.sin(ang)


def _lane_tables(cos_a, sin_a, cos_b, sin_b):
    c = np.concatenate([cos_a, cos_b, cos_a, cos_b], axis=-1)
    s = np.concatenate([-sin_a, -sin_b, sin_a, sin_b], axis=-1)
    return jnp.asarray(c), jnp.asarray(s)


def _interleave_pairs(w, width):
    lead = w.shape[:-1]
    w = w.reshape(lead + (-1, 2, 2, width // 2))
    return jnp.swapaxes(w, -3, -2).reshape(lead + (-1,))


def _pick_tile(seq, want):
    t = min(seq, want)
    assert seq % t == 0 and t % HALO == 0
    return t


def kernel(x, norm1_g, w_in, mla_q_norm_g, mla_w_q_up, mla_kv_norm_g, mla_w_kv_up, gqa_q_norm_g,
           gqa_k_norm_g, group_norm_mla_g, group_norm_gqa_g, w_out, norm2_g, ffn_w_up, ffn_conv_w,
           ffn_conv_b, ffn_w_down, final_norm_g):
    batch, seq, d_model = x.shape
    depth = w_in.shape[0]
    bf = jnp.bfloat16
    tm_proj = _pick_tile(seq, 1024)
    tq = _pick_tile(seq, 512)
    tm_ffn = _pick_tile(seq, 512)

    t = np.arange(seq)
    cos1, sin1 = _rope_np(t, MLA_ROPE)
    cos_r, sin_r = _rope_np(t // GRID_W, GQA_HEAD_DIM // 2)
    cos_c, sin_c = _rope_np(t % GRID_W, GQA_HEAD_DIM // 2)
    tables = _lane_tables(cos1, sin1, cos1, sin1) + _lane_tables(cos_r, sin_r, cos_c, sin_c)

    row = lambda v: v.reshape(1, -1)
    h2 = x.reshape(batch * seq, d_model)
    for l in range(depth):
        kpe_w = w_in[l][:, OFF_KPE:OFF_GQ]
        win = jnp.concatenate(
            [w_in[l][:, :OFF_KPE],
             _interleave_pairs(jnp.concatenate([kpe_w, kpe_w], axis=1), MLA_ROPE),
             _interleave_pairs(w_in[l][:, OFF_GQ:OFF_GV], GQA_HEAD_DIM // 2)],
            axis=1).astype(bf)
        winvt = w_in[l][:, OFF_GV:IN_WIDTH].T.astype(bf)
        wq3 = mla_w_q_up[l].reshape(MLA_Q_RANK, MLA_HEADS, MLA_QK)
        wq = jnp.concatenate(
            [wq3[:, :, :MLA_NOPE].reshape(MLA_Q_RANK, -1),
             _interleave_pairs(wq3[:, :, MLA_NOPE:].reshape(MLA_Q_RANK, -1), MLA_ROPE)],
            axis=1).astype(bf)
        wkv3 = mla_w_kv_up[l].reshape(MLA_KV_RANK, MLA_HEADS, MLA_NOPE + MLA_V)
        wkvk = wkv3[:, :, :MLA_NOPE].reshape(MLA_KV_RANK, -1).astype(bf)
        wkvvt = wkv3[:, :, MLA_NOPE:].reshape(MLA_KV_RANK, -1).T.astype(bf)
        qm, km, qg, kg, vt = _projections(
            h2, row(norm1_g[l]), win, winvt, row(mla_q_norm_g[l]), wq, row(mla_kv_norm_g[l]),
            wkvk, wkvvt, row(_interleave_pairs(gqa_q_norm_g[l], GQA_HEAD_DIM // 2)),
            row(_interleave_pairs(gqa_k_norm_g[l], GQA_HEAD_DIM // 2)), tables, batch, seq, tm_proj)
        h2 = _attention(qm, qg, km, kg, vt, h2, row(group_norm_mla_g[l]),
                        row(group_norm_gqa_g[l]), w_out[l].astype(bf), batch, seq, tq)
        h2 = _conv_ffn(h2, row(norm2_g[l]), ffn_w_up[l].astype(bf), ffn_conv_w[l],
                       row(ffn_conv_b[l]), ffn_w_down[l].astype(bf), row(final_norm_g), seq, tm_ffn,
                       final_norm=(l == depth - 1))
    return h2.reshape(batch, seq, d_model)
```

```python
import functools

import numpy as np

import jax
import jax.numpy as jnp
from jax import lax
from jax.experimental import pallas as pl
from jax.experimental.pallas import tpu as pltpu

EPS = 1e-6
LOG2_E = 1.4426950408889634
ROPE_THETA = 10000.0
GRID_W = 64

D_MODEL = 1024
MLA_HEADS = 4
MLA_Q_RANK = 384
MLA_KV_RANK = 256
MLA_NOPE = 128
MLA_ROPE = 64
MLA_V = 128
MLA_QK = MLA_NOPE + MLA_ROPE

GQA_HEADS = 4
GQA_KV_HEADS = 2
GQA_HEAD_DIM = 128
GQA_GROUP = GQA_HEADS // GQA_KV_HEADS

D_FF = 2816
LANES = 128
BF16_ROWS = 16

OFF_KPE = MLA_Q_RANK + MLA_KV_RANK
OFF_GQ = OFF_KPE + MLA_ROPE
OFF_GK = OFF_GQ + GQA_HEADS * GQA_HEAD_DIM
OFF_GV = OFF_GK + GQA_KV_HEADS * GQA_HEAD_DIM
IN_WIDTH = OFF_GV + GQA_KV_HEADS * GQA_HEAD_DIM

Z_CQ = 0
Z_CKV = Z_CQ + MLA_Q_RANK
Z_KPE = Z_CKV + MLA_KV_RANK
Z_GQ = Z_KPE + LANES
Z_GK = Z_GQ + GQA_HEADS * GQA_HEAD_DIM
Z_WIDTH = Z_GK + GQA_KV_HEADS * GQA_HEAD_DIM

N_V_HEADS = MLA_HEADS + GQA_KV_HEADS
VT_ROWS = MLA_V + BF16_ROWS

FF_CHUNK = 256
N_FF_CHUNKS = D_FF // FF_CHUNK
HALO = 16
KEY_CHUNK = 1024
PV_DOTS = 4
PV_LAG = 2
PROJ_SPLIT = 4

VMEM_LIMIT = 56 * 1024 * 1024


def _rms(x, g):
    y = x * lax.rsqrt(jnp.mean(x * x, axis=-1, keepdims=True) + EPS)
    return y * g


def _rope(x, c, s):
    return x * c + pltpu.roll(x, LANES // 2, 1) * s


def _dot(a, b):
    return jnp.dot(a, b, preferred_element_type=jnp.float32)


def _dot_t(a, b):
    return lax.dot_general(a, b, (((1,), (1,)), ((), ())), preferred_element_type=jnp.float32)


def _proj_kernel(x_ref, g1_ref, win_ref, winvt_ref, gq_ref, wq_ref, gkv_ref, wkvk_ref, wkvvt_ref,
                 ggq_ref, ggk_ref, c1_ref, s1_ref, cg_ref, sg_ref, *refs):
    n_casts = (len(refs) - 5) // 2
    qm_ref, km_ref, qg_ref, kg_ref, vt_ref = refs[n_casts:n_casts + 5]
    bf = jnp.bfloat16
    for src, dst in zip(refs[:n_casts], refs[n_casts + 5:]):
        dst[...] = src[...].astype(bf)
    tm = x_ref.shape[0]
    rows_per = tm // PROJ_SPLIT
    scale_m = MLA_QK ** -0.5 * LOG2_E
    scale_g = GQA_HEAD_DIM ** -0.5 * LOG2_E
    nope_w = MLA_HEADS * MLA_NOPE

    for r in range(PROJ_SPLIT):
        rows = pl.ds(r * rows_per, rows_per)
        cols = slice(r * rows_per, (r + 1) * rows_per)
        u = _rms(x_ref[rows, :], g1_ref[...]).astype(bf)
        z = _dot(u, win_ref[...])
        c_q = _rms(z[:, Z_CQ:Z_CKV], gq_ref[...]).astype(bf)
        q = _dot(c_q, wq_ref[...])
        c_kv = _rms(z[:, Z_CKV:Z_KPE], gkv_ref[...]).astype(bf)
        k_nope = _dot(c_kv, wkvk_ref[...])
        vt_mla = _dot_t(wkvvt_ref[...], c_kv)
        vt_gqa = _dot_t(winvt_ref[...], u)

        c1, s1 = c1_ref[rows, :], s1_ref[rows, :]
        qm_ref[rows, :nope_w] = (q[:, :nope_w] * scale_m).astype(bf)
        for i in range(MLA_HEADS // 2):
            lo = nope_w + LANES * i
            qm_ref[rows, lo:lo + LANES] = (_rope(q[:, lo:lo + LANES], c1, s1) * scale_m).astype(bf)
        km_ref[rows, :nope_w] = k_nope.astype(bf)
        km_ref[rows, nope_w:] = _rope(z[:, Z_KPE:Z_GQ], c1, s1).astype(bf)
        for h in range(MLA_HEADS):
            vt_ref[VT_ROWS * h:VT_ROWS * h + MLA_V, cols] = vt_mla[MLA_V * h:MLA_V * (h + 1)].astype(bf)

        cg, sg = cg_ref[rows, :], sg_ref[rows, :]
        for h in range(GQA_HEADS):
            t = _rms(z[:, Z_GQ + LANES * h:Z_GQ + LANES * (h + 1)], ggq_ref[...])
            qg_ref[rows, LANES * h:LANES * (h + 1)] = (_rope(t, cg, sg) * scale_g).astype(bf)
        for j in range(GQA_KV_HEADS):
            t = _rms(z[:, Z_GK + LANES * j:Z_GK + LANES * (j + 1)], ggk_ref[...])
            kg_ref[rows, LANES * j:LANES * (j + 1)] = _rope(t, cg, sg).astype(bf)
            lo = VT_ROWS * (MLA_HEADS + j)
            vt_ref[lo:lo + MLA_V, cols] = vt_gqa[LANES * j:LANES * (j + 1)].astype(bf)

    ones = jnp.ones((BF16_ROWS, tm), bf)
    for h in range(N_V_HEADS):
        vt_ref[VT_ROWS * h + MLA_V:VT_ROWS * (h + 1), :] = ones


def _cast_block_spec(rows, cols, n_steps, batch):
    n_blocks = max(d for d in range(1, n_steps + 1) if (rows // BF16_ROWS) % d == 0)
    return pl.BlockSpec((rows // n_blocks, cols),
                        lambda si, b: (jnp.minimum(si * batch + b, n_blocks - 1), 0))


def _projections(x2, g1, win, winvt, gq, wq, gkv, wkvk, wkvvt, ggq, ggk, tables, casts,
                 batch, seq, tm):
    tokens, d_model = x2.shape
    ns = seq // tm
    row = lambda si, b: (b * ns + si, 0)
    const = lambda si, b: (0, 0)
    tab = lambda si, b: (si, 0)

    def full(a):
        return pl.BlockSpec(a.shape, const)

    bf = jnp.bfloat16
    cast_specs = [_cast_block_spec(w.shape[0], w.shape[1], ns * batch, batch) for w in casts]
    widths = (MLA_HEADS * MLA_NOPE + LANES * (MLA_HEADS // 2), MLA_HEADS * MLA_NOPE + LANES,
              GQA_HEADS * GQA_HEAD_DIM, GQA_KV_HEADS * GQA_HEAD_DIM)
    vt_rows = N_V_HEADS * VT_ROWS
    return pl.pallas_call(
        _proj_kernel,
        grid=(ns, batch),
        in_specs=[pl.BlockSpec((tm, d_model), row), full(g1), full(win), full(winvt), full(gq),
                  full(wq), full(gkv), full(wkvk), full(wkvvt), full(ggq), full(ggk)]
                 + [pl.BlockSpec((tm, LANES), tab) for _ in tables] + cast_specs,
        out_specs=[pl.BlockSpec((tm, w), row) for w in widths]
                  + [pl.BlockSpec((None, vt_rows, tm), lambda si, b: (b, 0, si))] + cast_specs,
        out_shape=[jax.ShapeDtypeStruct((tokens, w), bf) for w in widths]
                  + [jax.ShapeDtypeStruct((batch, vt_rows, seq), bf)]
                  + [jax.ShapeDtypeStruct(w.shape, bf) for w in casts],
        compiler_params=pltpu.CompilerParams(
            dimension_semantics=("arbitrary", "arbitrary"), vmem_limit_bytes=VMEM_LIMIT),
        name="proj",
    )(x2, g1, win, winvt, gq, wq, gkv, wkvk, wkvvt, ggq, ggk, *tables, *casts)


def _attn_kernel(qm_ref, qg_ref, km_ref, kg_ref, vt_ref, x_ref, gnm_ref, gng_ref, wo_ref, h_ref,
                 st_ref, p_ref, ot_ref, o_ref):
    tq = qm_ref.shape[0]
    seq = km_ref.shape[0]
    kc = min(KEY_CHUNK, seq)
    n_chunks = seq // kc
    bf = jnp.bfloat16
    nope_w = MLA_HEADS * MLA_NOPE
    lane = lax.broadcasted_iota(jnp.int32, (1, LANES), 1)
    first_of_pair = (lane % (LANES // 2)) < (MLA_ROPE // 2)

    def mla_operands(h, keys):
        own = first_of_pair if h % 2 == 0 else jnp.logical_not(first_of_pair)
        rope_lo = nope_w + LANES * (h // 2)
        q_rope = jnp.where(own, qm_ref[:, rope_lo:rope_lo + LANES], jnp.zeros((), bf))
        q = jnp.concatenate([qm_ref[:, LANES * h:LANES * (h + 1)], q_rope], axis=1)
        k = jnp.concatenate([km_ref[keys, LANES * h:LANES * (h + 1)],
                             km_ref[keys, nope_w:nope_w + LANES]], axis=1)
        return k, q

    def gqa_operands(h, keys):
        j = h // GQA_GROUP
        return kg_ref[keys, LANES * j:LANES * (j + 1)], qg_ref[:, LANES * h:LANES * (h + 1)]

    units = ([(mla_operands, h, VT_ROWS * h) for h in range(MLA_HEADS)]
             + [(gqa_operands, h, VT_ROWS * (MLA_HEADS + h // GQA_GROUP)) for h in range(GQA_HEADS)])
    n_units = len(units)
    col_max = [None] * n_units

    def scores(u, c):
        operands, h, _ = units[u]
        s = _dot_t(*operands(h, slice(kc * c, kc * (c + 1))))
        st_ref[u % 2, kc * c:kc * (c + 1), :] = s
        cm = jnp.max(s.reshape(kc // 8, 8, tq), axis=0)
        col_max[u] = cm if c == 0 else jnp.maximum(col_max[u], cm)
        if c == n_chunks - 1:
            col_max[u] = jnp.broadcast_to(jnp.max(col_max[u], axis=0, keepdims=True), (8, tq))

    def probs(u, c):
        s = st_ref[u % 2, kc * c:kc * (c + 1), :].reshape(kc // 8, 8, tq)
        p = jnp.exp2(s - col_max[u][None]).reshape(kc, tq)
        p_ref[u % 2, kc * c:kc * (c + 1), :] = p.astype(bf)

    pv_every = max(n_chunks // PV_DOTS, 1)

    def weighted_values(u, c):
        if (c + 1) % pv_every:
            return
        vt_lo = units[u][2]
        keys = slice(kc * (c + 1 - pv_every), kc * (c + 1))
        pv = _dot(vt_ref[vt_lo:vt_lo + VT_ROWS, keys], p_ref[u % 2, keys, :])
        if c + 1 != pv_every:
            pv += ot_ref[u % 2]
        if c != n_chunks - 1:
            ot_ref[u % 2] = pv
        else:
            o_ref[:, LANES * u:LANES * (u + 1)] = (pv[:MLA_V] / pv[MLA_V:MLA_V + 1]).T

    for t in range(n_units + PV_LAG):
        for c in range(n_chunks):
            if t < n_units:
                scores(t, c)
            if 0 <= t - 1 < n_units:
                probs(t - 1, c)
            if 0 <= t - PV_LAG < n_units:
                weighted_values(t - PV_LAG, c)

    mla_w = MLA_HEADS * MLA_V
    o = jnp.concatenate([_rms(o_ref[:, :mla_w], gnm_ref[...]),
                         _rms(o_ref[:, mla_w:], gng_ref[...])], axis=-1)
    h_ref[...] = x_ref[...] + _dot(o.astype(bf), wo_ref[...])


def _attention(qm, qg, km, kg, vt, x2, gnm, gng, wo, batch, seq, tq):
    tokens, d_model = x2.shape
    nq = seq // tq
    row = lambda b, qi: (b * nq + qi, 0)
    per_batch = lambda b, qi: (b, 0)
    const = lambda b, qi: (0, 0)

    def full(a):
        return pl.BlockSpec(a.shape, const)

    return pl.pallas_call(
        _attn_kernel,
        grid=(batch, nq),
        in_specs=[pl.BlockSpec((tq, qm.shape[1]), row), pl.BlockSpec((tq, qg.shape[1]), row),
                  pl.BlockSpec((seq, km.shape[1]), per_batch),
                  pl.BlockSpec((seq, kg.shape[1]), per_batch),
                  pl.BlockSpec((None, vt.shape[1], seq), lambda b, qi: (b, 0, 0)),
                  pl.BlockSpec((tq, d_model), row), full(gnm), full(gng), full(wo)],
        out_specs=pl.BlockSpec((tq, d_model), row),
        out_shape=jax.ShapeDtypeStruct((tokens, d_model), jnp.float32),
        scratch_shapes=[pltpu.VMEM((2, seq, tq), jnp.float32),
                        pltpu.VMEM((2, seq, tq), jnp.bfloat16),
                        pltpu.VMEM((2, VT_ROWS, tq), jnp.float32),
                        pltpu.VMEM((tq, d_model), jnp.float32)],
        compiler_params=pltpu.CompilerParams(
            dimension_semantics=("arbitrary", "arbitrary"), vmem_limit_bytes=VMEM_LIMIT),
        name="attn",
    )(qm, qg, km, kg, vt, x2, gnm, gng, wo)


def _ffn_kernel(h_ref, hp_ref, hn_ref, g2_ref, wup_ref, cw_ref, cb_ref, wdn_ref, gf_ref,
                o_ref, u_ref, hu_ref, act_ref, *, tiles_per_seq, final_norm):
    tm = h_ref.shape[0]
    bf = jnp.bfloat16
    i = pl.program_id(0)
    si = i % tiles_per_seq
    keep_prev = (si != 0).astype(jnp.float32)
    keep_next = (si != tiles_per_seq - 1).astype(jnp.float32)
    g2 = g2_ref[...]
    u_ref[0:HALO, :] = (_rms(hp_ref[...], g2) * keep_prev).astype(bf)
    u_ref[HALO:HALO + tm, :] = _rms(h_ref[...], g2).astype(bf)
    u_ref[HALO + tm:, :] = (_rms(hn_ref[...], g2) * keep_next).astype(bf)
    lane_tiles = FF_CHUNK // LANES

    def col(part, c, j=0):
        return part * D_FF + FF_CHUNK * c + LANES * j

    def up(c, part):
        t = _dot(u_ref[...], wup_ref[:, col(part, c):col(part, c) + FF_CHUNK])
        for j in range(lane_tiles):
            hu_ref[c % 2, part, j] = t[:, LANES * j:LANES * (j + 1)]

    def conv(c, part, j):
        lo = col(part, c, j)
        w = cw_ref[:, lo:lo + LANES]
        src = hu_ref.at[c % 2, part, j]
        return (src[pl.ds(HALO - 1, tm), :] * w[0:1] + src[pl.ds(HALO, tm), :] * w[1:2]
                + src[pl.ds(HALO + 1, tm), :] * w[2:3] + cb_ref[:, lo:lo + LANES])

    def act(c, j):
        gate = conv(c, 0, j)
        lo = col(0, c, j)
        act_ref[:, lo:lo + LANES] = (gate * jax.nn.sigmoid(gate) * conv(c, 1, j)).astype(bf)

    assert lane_tiles == 2
    up(0, 0)
    up(0, 1)
    for c in range(N_FF_CHUNKS):
        for half in range(2):
            if c + 1 < N_FF_CHUNKS:
                up(c + 1, half)
            act(c, half)
    out = h_ref[...] + _dot(act_ref[...], wdn_ref[...])
    if final_norm:
        out = _rms(out, gf_ref[...])
    o_ref[...] = out


def _conv_ffn(h2, g2, wup, cw, cb, wdn, gf, seq, tm, final_norm):
    tokens, d_model = h2.shape
    tiles_per_seq = seq // tm
    n_halo_blocks = tokens // HALO
    per_tile = tm // HALO
    const = lambda i: (0, 0)
    kern = functools.partial(_ffn_kernel, tiles_per_seq=tiles_per_seq, final_norm=final_norm)
    return pl.pallas_call(
        kern,
        grid=(tokens // tm,),
        in_specs=[pl.BlockSpec((tm, d_model), lambda i: (i, 0)),
                  pl.BlockSpec((HALO, d_model), lambda i: (jnp.maximum(i * per_tile - 1, 0), 0)),
                  pl.BlockSpec((HALO, d_model),
                               lambda i: (jnp.minimum((i + 1) * per_tile, n_halo_blocks - 1), 0)),
                  pl.BlockSpec(g2.shape, const),
                  pl.BlockSpec(wup.shape, const, pipeline_mode=pl.Buffered(1)),
                  pl.BlockSpec(cw.shape, const), pl.BlockSpec(cb.shape, const),
                  pl.BlockSpec(wdn.shape, const, pipeline_mode=pl.Buffered(1)),
                  pl.BlockSpec(gf.shape, const)],
        out_specs=pl.BlockSpec((tm, d_model), lambda i: (i, 0)),
        out_shape=jax.ShapeDtypeStruct((tokens, d_model), jnp.float32),
        scratch_shapes=[pltpu.VMEM((tm + 2 * HALO, d_model), jnp.bfloat16),
                        pltpu.VMEM((2, 2, FF_CHUNK // LANES, tm + 2 * HALO, LANES), jnp.float32),
                        pltpu.VMEM((tm, D_FF), jnp.bfloat16)],
        compiler_params=pltpu.CompilerParams(
            dimension_semantics=("arbitrary",), vmem_limit_bytes=VMEM_LIMIT),
        name="convffn",
    )(h2, h2, h2, g2, wup, cw, cb, wdn, gf)


def _rope_np(pos, dim):
    inv = np.float32(ROPE_THETA) ** (-np.arange(0, dim, 2, dtype=np.float32) / np.float32(dim))
    ang = pos.astype(np.float32)[:, None] * inv[None, :]
    return np.cos(ang), np.sin(ang)


def _lane_tables(cos_a, sin_a, cos_b, sin_b):
    c = np.concatenate([cos_a, cos_b, cos_a, cos_b], axis=-1)
    s = np.concatenate([-sin_a, -sin_b, sin_a, sin_b], axis=-1)
    return jnp.asarray(c), jnp.asarray(s)


def _interleave_pairs(w, width):
    lead = w.shape[:-1]
    w = w.reshape(lead + (-1, 2, 2, width // 2))
    return jnp.swapaxes(w, -3, -2).reshape(lead + (-1,))


def _pick_tile(seq, want):
    t = min(seq, want)
    assert seq % t == 0 and t % HALO == 0
    return t


def kernel(x, norm1_g, w_in, mla_q_norm_g, mla_w_q_up, mla_kv_norm_g, mla_w_kv_up, gqa_q_norm_g,
           gqa_k_norm_g, group_norm_mla_g, group_norm_gqa_g, w_out, norm2_g, ffn_w_up, ffn_conv_w,
           ffn_conv_b, ffn_w_down, final_norm_g):
    batch, seq, d_model = x.shape
    depth = w_in.shape[0]
    bf = jnp.bfloat16
    tm_proj = _pick_tile(seq, 1024)
    tq = _pick_tile(seq, 512)
    tm_ffn = _pick_tile(seq, 512)

    t = np.arange(seq)
    cos1, sin1 = _rope_np(t, MLA_ROPE)
    cos_r, sin_r = _rope_np(t // GRID_W, GQA_HEAD_DIM // 2)
    cos_c, sin_c = _rope_np(t % GRID_W, GQA_HEAD_DIM // 2)
    tables = _lane_tables(cos1, sin1, cos1, sin1) + _lane_tables(cos_r, sin_r, cos_c, sin_c)

    row = lambda v: v.reshape(1, -1)
    h2 = x.reshape(batch * seq, d_model)
    for l in range(depth):
        kpe_w = w_in[l][:, OFF_KPE:OFF_GQ]
        win = jnp.concatenate(
            [w_in[l][:, :OFF_KPE],
             _interleave_pairs(jnp.concatenate([kpe_w, kpe_w], axis=1), MLA_ROPE),
             _interleave_pairs(w_in[l][:, OFF_GQ:OFF_GV], GQA_HEAD_DIM // 2)],
            axis=1).astype(bf)
        winvt = w_in[l][:, OFF_GV:IN_WIDTH].T.astype(bf)
        wq3 = mla_w_q_up[l].reshape(MLA_Q_RANK, MLA_HEADS, MLA_QK)
        wq = jnp.concatenate(
            [wq3[:, :, :MLA_NOPE].reshape(MLA_Q_RANK, -1),
             _interleave_pairs(wq3[:, :, MLA_NOPE:].reshape(MLA_Q_RANK, -1), MLA_ROPE)],
            axis=1).astype(bf)
        wkv3 = mla_w_kv_up[l].reshape(MLA_KV_RANK, MLA_HEADS, MLA_NOPE + MLA_V)
        wkvk = wkv3[:, :, :MLA_NOPE].reshape(MLA_KV_RANK, -1).astype(bf)
        wkvvt = wkv3[:, :, MLA_NOPE:].reshape(MLA_KV_RANK, -1).T.astype(bf)
        qm, km, qg, kg, vt, wo, wup, wdn = _projections(
            h2, row(norm1_g[l]), win, winvt, row(mla_q_norm_g[l]), wq, row(mla_kv_norm_g[l]),
            wkvk, wkvvt, row(_interleave_pairs(gqa_q_norm_g[l], GQA_HEAD_DIM // 2)),
            row(_interleave_pairs(gqa_k_norm_g[l], GQA_HEAD_DIM // 2)), tables,
            (w_out[l], ffn_w_up[l], ffn_w_down[l]), batch, seq, tm_proj)
        h2 = _attention(qm, qg, km, kg, vt, h2, row(group_norm_mla_g[l]),
                        row(group_norm_gqa_g[l]), wo, batch, seq, tq)
        h2 = _conv_ffn(h2, row(norm2_g[l]), wup, ffn_conv_w[l], row(ffn_conv_b[l]), wdn,
                       row(final_norm_g), seq, tm_ffn, final_norm=(l == depth - 1))
    return h2.reshape(batch, seq, d_model)
```

```python
import functools

import numpy as np

import jax
import jax.numpy as jnp
from jax import lax
from jax.experimental import pallas as pl
from jax.experimental.pallas import tpu as pltpu

EPS = 1e-6
LOG2_E = 1.4426950408889634
ROPE_THETA = 10000.0
GRID_W = 64

D_MODEL = 1024
MLA_HEADS = 4
MLA_Q_RANK = 384
MLA_KV_RANK = 256
MLA_NOPE = 128
MLA_ROPE = 64
MLA_V = 128
MLA_QK = MLA_NOPE + MLA_ROPE

GQA_HEADS = 4
GQA_KV_HEADS = 2
GQA_HEAD_DIM = 128
GQA_GROUP = GQA_HEADS // GQA_KV_HEADS

D_FF = 2816
LANES = 128
BF16_ROWS = 16

OFF_KPE = MLA_Q_RANK + MLA_KV_RANK
OFF_GQ = OFF_KPE + MLA_ROPE
OFF_GK = OFF_GQ + GQA_HEADS * GQA_HEAD_DIM
OFF_GV = OFF_GK + GQA_KV_HEADS * GQA_HEAD_DIM
IN_WIDTH = OFF_GV + GQA_KV_HEADS * GQA_HEAD_DIM

Z_CQ = 0
Z_CKV = Z_CQ + MLA_Q_RANK
Z_KPE = Z_CKV + MLA_KV_RANK
Z_GK = Z_KPE + LANES
Z_WIDTH = Z_GK + GQA_KV_HEADS * GQA_HEAD_DIM
ZT_GQ = 0
ZT_GV = ZT_GQ + GQA_HEADS * GQA_HEAD_DIM

N_V_HEADS = MLA_HEADS + GQA_KV_HEADS
VT_ROWS = MLA_V + BF16_ROWS

FF_CHUNK = 256
N_FF_CHUNKS = D_FF // FF_CHUNK
HALO = 16
KEY_CHUNK = 1024
PV_DOTS = 4
PV_LAG = 2
PROJ_SPLIT = 4

VMEM_LIMIT = 56 * 1024 * 1024


def _rms(x, g):
    y = x * lax.rsqrt(jnp.mean(x * x, axis=-1, keepdims=True) + EPS)
    return y * g


def _rope(x, c, s):
    return x * c + pltpu.roll(x, LANES // 2, 1) * s


def _dot(a, b):
    return jnp.dot(a, b, preferred_element_type=jnp.float32)


def _dot_t(a, b):
    return lax.dot_general(a, b, (((1,), (1,)), ((), ())), preferred_element_type=jnp.float32)


def _rope_t(x, c, s):
    half = x.shape[0] // 2
    return x * c + jnp.concatenate([x[half:], x[:half]], axis=0) * s


def _proj_kernel(x_ref, g1_ref, win_ref, wint_ref, gq_ref, wqt_ref, gkv_ref, wkvk_ref, wkvvt_ref,
                 ggq_ref, ggk_ref, c1_ref, s1_ref, cg_ref, sg_ref, c1t_ref, s1t_ref, cgt_ref,
                 sgt_ref, *refs):
    n_casts = (len(refs) - 5) // 2
    km_ref, kg_ref, qm_ref, qg_ref, vt_ref = refs[n_casts:n_casts + 5]
    bf = jnp.bfloat16
    for src, dst in zip(refs[:n_casts], refs[n_casts + 5:]):
        dst[...] = src[...].astype(bf)
    tm = x_ref.shape[0]
    rows_per = tm // PROJ_SPLIT
    scale_m = MLA_QK ** -0.5 * LOG2_E
    scale_g = GQA_HEAD_DIM ** -0.5 * LOG2_E
    nope_w = MLA_HEADS * MLA_NOPE

    for r in range(PROJ_SPLIT):
        rows = pl.ds(r * rows_per, rows_per)
        cols = slice(r * rows_per, (r + 1) * rows_per)
        u = _rms(x_ref[rows, :], g1_ref[...]).astype(bf)
        z = _dot(u, win_ref[...])
        zt = _dot_t(wint_ref[...], u)
        c_q = _rms(z[:, Z_CQ:Z_CKV], gq_ref[...]).astype(bf)
        qt = _dot_t(wqt_ref[...], c_q)
        c_kv = _rms(z[:, Z_CKV:Z_KPE], gkv_ref[...]).astype(bf)
        k_nope = _dot(c_kv, wkvk_ref[...])
        vt_mla = _dot_t(wkvvt_ref[...], c_kv)

        c1t, s1t = c1t_ref[:, cols], s1t_ref[:, cols]
        qm_ref[:nope_w, cols] = (qt[:nope_w] * scale_m).astype(bf)
        for i in range(MLA_HEADS // 2):
            lo = nope_w + LANES * i
            qm_ref[lo:lo + LANES, cols] = (_rope_t(qt[lo:lo + LANES], c1t, s1t) * scale_m).astype(bf)
        km_ref[rows, :nope_w] = k_nope.astype(bf)
        km_ref[rows, nope_w:] = _rope(z[:, Z_KPE:Z_GK], c1_ref[rows, :], s1_ref[rows, :]).astype(bf)
        for h in range(MLA_HEADS):
            vt_ref[VT_ROWS * h:VT_ROWS * h + MLA_V, cols] = vt_mla[MLA_V * h:MLA_V * (h + 1)].astype(bf)

        cgt, sgt = cgt_ref[:, cols], sgt_ref[:, cols]
        gain_t = jnp.concatenate([ggq_ref[...]] * (rows_per // LANES), axis=1)
        for h in range(GQA_HEADS):
            t = zt[ZT_GQ + LANES * h:ZT_GQ + LANES * (h + 1)]
            t = t * lax.rsqrt(jnp.mean(t * t, axis=0, keepdims=True) + EPS) * gain_t
            qg_ref[LANES * h:LANES * (h + 1), cols] = (_rope_t(t, cgt, sgt) * scale_g).astype(bf)
        cg, sg = cg_ref[rows, :], sg_ref[rows, :]
        for j in range(GQA_KV_HEADS):
            t = _rms(z[:, Z_GK + LANES * j:Z_GK + LANES * (j + 1)], ggk_ref[...])
            kg_ref[rows, LANES * j:LANES * (j + 1)] = _rope(t, cg, sg).astype(bf)
            lo = VT_ROWS * (MLA_HEADS + j)
            vt_ref[lo:lo + MLA_V, cols] = zt[ZT_GV + LANES * j:ZT_GV + LANES * (j + 1)].astype(bf)

    ones = jnp.ones((BF16_ROWS, tm), bf)
    for h in range(N_V_HEADS):
        vt_ref[VT_ROWS * h + MLA_V:VT_ROWS * (h + 1), :] = ones


def _cast_block_spec(rows, cols, n_steps, batch):
    n_blocks = max(d for d in range(1, n_steps + 1) if (rows // BF16_ROWS) % d == 0)
    return pl.BlockSpec((rows // n_blocks, cols),
                        lambda si, b: (jnp.minimum(si * batch + b, n_blocks - 1), 0))


def _projections(x2, g1, win, wint, gq, wqt, gkv, wkvk, wkvvt, ggq, ggk, tables, tables_t, casts,
                 batch, seq, tm):
    tokens, d_model = x2.shape
    ns = seq // tm
    row = lambda si, b: (b * ns + si, 0)
    const = lambda si, b: (0, 0)
    tab = lambda si, b: (si, 0)

    def full(a):
        return pl.BlockSpec(a.shape, const)

    bf = jnp.bfloat16
    cast_specs = [_cast_block_spec(w.shape[0], w.shape[1], ns * batch, batch) for w in casts]
    k_widths = (MLA_HEADS * MLA_NOPE + LANES, GQA_KV_HEADS * GQA_HEAD_DIM)
    t_rows = (MLA_HEADS * MLA_NOPE + LANES * (MLA_HEADS // 2), GQA_HEADS * GQA_HEAD_DIM,
              N_V_HEADS * VT_ROWS)
    t_spec = lambda r: pl.BlockSpec((None, r, tm), lambda si, b: (b, 0, si))
    return pl.pallas_call(
        _proj_kernel,
        grid=(ns, batch),
        in_specs=[pl.BlockSpec((tm, d_model), row), full(g1), full(win), full(wint), full(gq),
                  full(wqt), full(gkv), full(wkvk), full(wkvvt), full(ggq), full(ggk)]
                 + [pl.BlockSpec((tm, LANES), tab) for _ in tables]
                 + [pl.BlockSpec((LANES, tm), lambda si, b: (0, si)) for _ in tables_t] + cast_specs,
        out_specs=[pl.BlockSpec((tm, w), row) for w in k_widths] + [t_spec(r) for r in t_rows]
                  + cast_specs,
        out_shape=[jax.ShapeDtypeStruct((tokens, w), bf) for w in k_widths]
                  + [jax.ShapeDtypeStruct((batch, r, seq), bf) for r in t_rows]
                  + [jax.ShapeDtypeStruct(w.shape, bf) for w in casts],
        compiler_params=pltpu.CompilerParams(
            dimension_semantics=("arbitrary", "arbitrary"), vmem_limit_bytes=VMEM_LIMIT),
        name="proj",
    )(x2, g1, win, wint, gq, wqt, gkv, wkvk, wkvvt, ggq, ggk, *tables, *tables_t, *casts)


def _attn_kernel(qm_ref, qg_ref, km_ref, kg_ref, vt_ref, x_ref, gnm_ref, gng_ref, wo_ref, h_ref,
                 st_ref, p_ref, ot_ref, o_ref):
    tq = qm_ref.shape[1]
    seq = km_ref.shape[0]
    kc = min(KEY_CHUNK, seq)
    n_chunks = seq // kc
    bf = jnp.bfloat16
    nope_w = MLA_HEADS * MLA_NOPE
    dim = lax.broadcasted_iota(jnp.int32, (LANES, tq), 0)
    first_of_pair = (dim % (LANES // 2)) < (MLA_ROPE // 2)

    def mla_operands(h, keys):
        own = first_of_pair if h % 2 == 0 else jnp.logical_not(first_of_pair)
        rope_lo = nope_w + LANES * (h // 2)
        q_rope = jnp.where(own, qm_ref[rope_lo:rope_lo + LANES, :], jnp.zeros((), bf))
        q_t = jnp.concatenate([qm_ref[LANES * h:LANES * (h + 1), :], q_rope], axis=0)
        k = jnp.concatenate([km_ref[keys, LANES * h:LANES * (h + 1)],
                             km_ref[keys, nope_w:nope_w + LANES]], axis=1)
        return k, q_t

    def gqa_operands(h, keys):
        j = h // GQA_GROUP
        return kg_ref[keys, LANES * j:LANES * (j + 1)], qg_ref[LANES * h:LANES * (h + 1), :]

    units = ([(mla_operands, h, VT_ROWS * h) for h in range(MLA_HEADS)]
             + [(gqa_operands, h, VT_ROWS * (MLA_HEADS + h // GQA_GROUP)) for h in range(GQA_HEADS)])
    n_units = len(units)
    col_max = [None] * n_units

    def scores(u, c):
        operands, h, _ = units[u]
        s = _dot(*operands(h, slice(kc * c, kc * (c + 1))))
        st_ref[u % 2, kc * c:kc * (c + 1), :] = s
        cm = jnp.max(s.reshape(kc // 8, 8, tq), axis=0)
        col_max[u] = cm if c == 0 else jnp.maximum(col_max[u], cm)
        if c == n_chunks - 1:
            col_max[u] = jnp.broadcast_to(jnp.max(col_max[u], axis=0, keepdims=True), (8, tq))

    def probs(u, c):
        s = st_ref[u % 2, kc * c:kc * (c + 1), :].reshape(kc // 8, 8, tq)
        p = jnp.exp2(s - col_max[u][None]).reshape(kc, tq)
        p_ref[u % 2, kc * c:kc * (c + 1), :] = p.astype(bf)

    pv_every = max(n_chunks // PV_DOTS, 1)

    def weighted_values(u, c):
        if (c + 1) % pv_every:
            return
        vt_lo = units[u][2]
        keys = slice(kc * (c + 1 - pv_every), kc * (c + 1))
        pv = _dot(vt_ref[vt_lo:vt_lo + VT_ROWS, keys], p_ref[u % 2, keys, :])
        if c + 1 != pv_every:
            pv += ot_ref[u % 2]
        if c != n_chunks - 1:
            ot_ref[u % 2] = pv
        else:
            o_ref[:, LANES * u:LANES * (u + 1)] = (pv[:MLA_V] / pv[MLA_V:MLA_V + 1]).T

    for t in range(n_units + PV_LAG):
        for c in range(n_chunks):
            if t < n_units:
                scores(t, c)
            if 0 <= t - 1 < n_units:
                probs(t - 1, c)
            if 0 <= t - PV_LAG < n_units:
                weighted_values(t - PV_LAG, c)

    mla_w = MLA_HEADS * MLA_V
    o = jnp.concatenate([_rms(o_ref[:, :mla_w], gnm_ref[...]),
                         _rms(o_ref[:, mla_w:], gng_ref[...])], axis=-1)
    h_ref[...] = x_ref[...] + _dot(o.astype(bf), wo_ref[...])


def _attention(qm, qg, km, kg, vt, x2, gnm, gng, wo, batch, seq, tq):
    tokens, d_model = x2.shape
    nq = seq // tq
    row = lambda b, qi: (b * nq + qi, 0)
    per_batch = lambda b, qi: (b, 0)
    const = lambda b, qi: (0, 0)

    def full(a):
        return pl.BlockSpec(a.shape, const)

    return pl.pallas_call(
        _attn_kernel,
        grid=(batch, nq),
        in_specs=[pl.BlockSpec((None, qm.shape[1], tq), lambda b, qi: (b, 0, qi)),
                  pl.BlockSpec((None, qg.shape[1], tq), lambda b, qi: (b, 0, qi)),
                  pl.BlockSpec((seq, km.shape[1]), per_batch),
                  pl.BlockSpec((seq, kg.shape[1]), per_batch),
                  pl.BlockSpec((None, vt.shape[1], seq), lambda b, qi: (b, 0, 0)),
                  pl.BlockSpec((tq, d_model), row), full(gnm), full(gng), full(wo)],
        out_specs=pl.BlockSpec((tq, d_model), row),
        out_shape=jax.ShapeDtypeStruct((tokens, d_model), jnp.float32),
        scratch_shapes=[pltpu.VMEM((2, seq, tq), jnp.float32),
                        pltpu.VMEM((2, seq, tq), jnp.bfloat16),
                        pltpu.VMEM((2, VT_ROWS, tq), jnp.float32),
                        pltpu.VMEM((tq, d_model), jnp.float32)],
        compiler_params=pltpu.CompilerParams(
            dimension_semantics=("arbitrary", "arbitrary"), vmem_limit_bytes=VMEM_LIMIT),
        name="attn",
    )(qm, qg, km, kg, vt, x2, gnm, gng, wo)


def _ffn_kernel(h_ref, hp_ref, hn_ref, g2_ref, wup_ref, cw_ref, cb_ref, wdn_ref, gf_ref,
                o_ref, u_ref, hu_ref, act_ref, *, tiles_per_seq, final_norm):
    tm = h_ref.shape[0]
    bf = jnp.bfloat16
    i = pl.program_id(0)
    si = i % tiles_per_seq
    keep_prev = (si != 0).astype(jnp.float32)
    keep_next = (si != tiles_per_seq - 1).astype(jnp.float32)
    g2 = g2_ref[...]
    u_ref[0:HALO, :] = (_rms(hp_ref[...], g2) * keep_prev).astype(bf)
    u_ref[HALO:HALO + tm, :] = _rms(h_ref[...], g2).astype(bf)
    u_ref[HALO + tm:, :] = (_rms(hn_ref[...], g2) * keep_next).astype(bf)
    lane_tiles = FF_CHUNK // LANES

    def col(part, c, j=0):
        return part * D_FF + FF_CHUNK * c + LANES * j

    def up(c, part):
        t = _dot(u_ref[...], wup_ref[:, col(part, c):col(part, c) + FF_CHUNK])
        for j in range(lane_tiles):
            hu_ref[c % 2, part, j] = t[:, LANES * j:LANES * (j + 1)]

    def conv(c, part, j):
        lo = col(part, c, j)
        w = cw_ref[:, lo:lo + LANES]
        src = hu_ref.at[c % 2, part, j]
        return (src[pl.ds(HALO - 1, tm), :] * w[0:1] + src[pl.ds(HALO, tm), :] * w[1:2]
                + src[pl.ds(HALO + 1, tm), :] * w[2:3] + cb_ref[:, lo:lo + LANES])

    def act(c, j):
        gate = conv(c, 0, j)
        lo = col(0, c, j)
        act_ref[:, lo:lo + LANES] = (gate * jax.nn.sigmoid(gate) * conv(c, 1, j)).astype(bf)

    assert lane_tiles == 2
    up(0, 0)
    up(0, 1)
    for c in range(N_FF_CHUNKS):
        for half in range(2):
            if c + 1 < N_FF_CHUNKS:
                up(c + 1, half)
            act(c, half)
    out = h_ref[...] + _dot(act_ref[...], wdn_ref[...])
    if final_norm:
        out = _rms(out, gf_ref[...])
    o_ref[...] = out


def _conv_ffn(h2, g2, wup, cw, cb, wdn, gf, seq, tm, final_norm):
    tokens, d_model = h2.shape
    tiles_per_seq = seq // tm
    n_halo_blocks = tokens // HALO
    per_tile = tm // HALO
    const = lambda i: (0, 0)
    kern = functools.partial(_ffn_kernel, tiles_per_seq=tiles_per_seq, final_norm=final_norm)
    return pl.pallas_call(
        kern,
        grid=(tokens // tm,),
        in_specs=[pl.BlockSpec((tm, d_model), lambda i: (i, 0)),
                  pl.BlockSpec((HALO, d_model), lambda i: (jnp.maximum(i * per_tile - 1, 0), 0)),
                  pl.BlockSpec((HALO, d_model),
                               lambda i: (jnp.minimum((i + 1) * per_tile, n_halo_blocks - 1), 0)),
                  pl.BlockSpec(g2.shape, const),
                  pl.BlockSpec(wup.shape, const, pipeline_mode=pl.Buffered(1)),
                  pl.BlockSpec(cw.shape, const), pl.BlockSpec(cb.shape, const),
                  pl.BlockSpec(wdn.shape, const, pipeline_mode=pl.Buffered(1)),
                  pl.BlockSpec(gf.shape, const)],
        out_specs=pl.BlockSpec((tm, d_model), lambda i: (i, 0)),
        out_shape=jax.ShapeDtypeStruct((tokens, d_model), jnp.float32),
        scratch_shapes=[pltpu.VMEM((tm + 2 * HALO, d_model), jnp.bfloat16),
                        pltpu.VMEM((2, 2, FF_CHUNK // LANES, tm + 2 * HALO, LANES), jnp.float32),
                        pltpu.VMEM((tm, D_FF), jnp.bfloat16)],
        compiler_params=pltpu.CompilerParams(
            dimension_semantics=("arbitrary",), vmem_limit_bytes=VMEM_LIMIT),
        name="convffn",
    )(h2, h2, h2, g2, wup, cw, cb, wdn, gf)


def _rope_np(pos, dim):
    inv = np.float32(ROPE_THETA) ** (-np.arange(0, dim, 2, dtype=np.float32) / np.float32(dim))
    ang = pos.astype(np.float32)[:, None] * inv[None, :]
    return np.cos(ang), np.sin(ang)


def _lane_tables(cos_a, sin_a, cos_b, sin_b):
    c = np.concatenate([cos_a, cos_b, cos_a, cos_b], axis=-1)
    s = np.concatenate([-sin_a, -sin_b, sin_a, sin_b], axis=-1)
    return c, s


def _interleave_pairs(w, width):
    lead = w.shape[:-1]
    w = w.reshape(lead + (-1, 2, 2, width // 2))
    return jnp.swapaxes(w, -3, -2).reshape(lead + (-1,))


def _pick_tile(seq, want):
    t = min(seq, want)
    assert seq % t == 0 and t % HALO == 0
    return t


def kernel(x, norm1_g, w_in, mla_q_norm_g, mla_w_q_up, mla_kv_norm_g, mla_w_kv_up, gqa_q_norm_g,
           gqa_k_norm_g, group_norm_mla_g, group_norm_gqa_g, w_out, norm2_g, ffn_w_up, ffn_conv_w,
           ffn_conv_b, ffn_w_down, final_norm_g):
    batch, seq, d_model = x.shape
    depth = w_in.shape[0]
    bf = jnp.bfloat16
    tm_proj = _pick_tile(seq, 1024)
    tq = _pick_tile(seq, 512)
    tm_ffn = _pick_tile(seq, 512)

    t = np.arange(seq)
    cos1, sin1 = _rope_np(t, MLA_ROPE)
    cos_r, sin_r = _rope_np(t // GRID_W, GQA_HEAD_DIM // 2)
    cos_c, sin_c = _rope_np(t % GRID_W, GQA_HEAD_DIM // 2)
    tables_np = _lane_tables(cos1, sin1, cos1, sin1) + _lane_tables(cos_r, sin_r, cos_c, sin_c)
    tables = tuple(jnp.asarray(t) for t in tables_np)
    tables_t = tuple(jnp.asarray(np.ascontiguousarray(t.T)) for t in tables_np)

    row = lambda v: v.reshape(1, -1)
    h2 = x.reshape(batch * seq, d_model)
    for l in range(depth):
        kpe_w = w_in[l][:, OFF_KPE:OFF_GQ]
        win = jnp.concatenate(
            [w_in[l][:, :OFF_KPE],
             _interleave_pairs(jnp.concatenate([kpe_w, kpe_w], axis=1), MLA_ROPE),
             _interleave_pairs(w_in[l][:, OFF_GK:OFF_GV], GQA_HEAD_DIM // 2)],
            axis=1).astype(bf)
        wint = jnp.concatenate(
            [_interleave_pairs(w_in[l][:, OFF_GQ:OFF_GK], GQA_HEAD_DIM // 2),
             w_in[l][:, OFF_GV:IN_WIDTH]], axis=1).T.astype(bf)
        wq3 = mla_w_q_up[l].reshape(MLA_Q_RANK, MLA_HEADS, MLA_QK)
        wqt = jnp.concatenate(
            [wq3[:, :, :MLA_NOPE].reshape(MLA_Q_RANK, -1),
             _interleave_pairs(wq3[:, :, MLA_NOPE:].reshape(MLA_Q_RANK, -1), MLA_ROPE)],
            axis=1).T.astype(bf)
        gain_q = _interleave_pairs(gqa_q_norm_g[l], GQA_HEAD_DIM // 2)
        gain_q = jnp.broadcast_to(gain_q[:, None], (GQA_HEAD_DIM, LANES))
        wkv3 = mla_w_kv_up[l].reshape(MLA_KV_RANK, MLA_HEADS, MLA_NOPE + MLA_V)
        wkvk = wkv3[:, :, :MLA_NOPE].reshape(MLA_KV_RANK, -1).astype(bf)
        wkvvt = wkv3[:, :, MLA_NOPE:].reshape(MLA_KV_RANK, -1).T.astype(bf)
        km, kg, qm, qg, vt, wo, wup, wdn = _projections(
            h2, row(norm1_g[l]), win, wint, row(mla_q_norm_g[l]), wqt, row(mla_kv_norm_g[l]),
            wkvk, wkvvt, gain_q, row(_interleave_pairs(gqa_k_norm_g[l], GQA_HEAD_DIM // 2)),
            tables, tables_t, (w_out[l], ffn_w_up[l], ffn_w_down[l]), batch, seq, tm_proj)
        h2 = _attention(qm, qg, km, kg, vt, h2, row(group_norm_mla_g[l]),
                        row(group_norm_gqa_g[l]), wo, batch, seq, tq)
        h2 = _conv_ffn(h2, row(norm2_g[l]), wup, ffn_conv_w[l], row(ffn_conv_b[l]), wdn,
                       row(final_norm_g), seq, tm_ffn, final_norm=(l == depth - 1))
    return h2.reshape(batch, seq, d_model)
```

```python
import functools

import numpy as np

import jax
import jax.numpy as jnp
from jax import lax
from jax.experimental import pallas as pl
from jax.experimental.pallas import tpu as pltpu

EPS = 1e-6
LOG2_E = 1.4426950408889634
ROPE_THETA = 10000.0
GRID_W = 64

D_MODEL = 1024
MLA_HEADS = 4
MLA_Q_RANK = 384
MLA_KV_RANK = 256
MLA_NOPE = 128
MLA_ROPE = 64
MLA_V = 128
MLA_QK = MLA_NOPE + MLA_ROPE

GQA_HEADS = 4
GQA_KV_HEADS = 2
GQA_HEAD_DIM = 128
GQA_GROUP = GQA_HEADS // GQA_KV_HEADS

D_FF = 2816
LANES = 128
BF16_ROWS = 16

OFF_KPE = MLA_Q_RANK + MLA_KV_RANK
OFF_GQ = OFF_KPE + MLA_ROPE
OFF_GK = OFF_GQ + GQA_HEADS * GQA_HEAD_DIM
OFF_GV = OFF_GK + GQA_KV_HEADS * GQA_HEAD_DIM
IN_WIDTH = OFF_GV + GQA_KV_HEADS * GQA_HEAD_DIM

Z_CQ = 0
Z_CKV = Z_CQ + MLA_Q_RANK
Z_KPE = Z_CKV + MLA_KV_RANK
Z_GQ = Z_KPE + LANES
Z_GK = Z_GQ + GQA_HEADS * GQA_HEAD_DIM
Z_WIDTH = Z_GK + GQA_KV_HEADS * GQA_HEAD_DIM

N_V_HEADS = MLA_HEADS + GQA_KV_HEADS
VT_ROWS = MLA_V + BF16_ROWS
KN_ROWS = 8
MIN_DENOM = 2.0 ** -60

FF_CHUNK = 256
N_FF_CHUNKS = D_FF // FF_CHUNK
HALO = 8
KEY_CHUNK = 1024
PV_LAG = 2
PROJ_SPLIT = 4

PROJ_ROWS = 1024
ATTN_QUERIES = 512
FFN_ROWS = 512

V7X_VMEM_BYTES = 64 * 1024 * 1024
VMEM_LIMIT = V7X_VMEM_BYTES * 7 // 8


def _rms(x, g):
    y = x * lax.rsqrt(jnp.mean(x * x, axis=-1, keepdims=True) + EPS)
    return y * g


def _rope(x, c, s):
    return x * c + pltpu.roll(x, LANES // 2, 1) * s


def _dot(a, b):
    return jnp.dot(a, b, preferred_element_type=jnp.float32)


def _dot_t(a, b):
    return lax.dot_general(a, b, (((1,), (1,)), ((), ())), preferred_element_type=jnp.float32)


def _proj_kernel(x_ref, g1_ref, win_ref, winvt_ref, gq_ref, wq_ref, gkv_ref, wkvk_ref, wkvvt_ref,
                 ggq_ref, ggk_ref, sel_ref, c1_ref, s1_ref, cg_ref, sg_ref, *refs):
    n_casts = (len(refs) - 6) // 2
    qm_ref, km_ref, qg_ref, kg_ref, vt_ref, kn2_ref = refs[n_casts:n_casts + 6]
    bf = jnp.bfloat16
    key_norm2 = None
    for src, dst in zip(refs[:n_casts], refs[n_casts + 6:]):
        dst[...] = src[...].astype(bf)
    tm = x_ref.shape[0]
    rows_per = tm // PROJ_SPLIT
    scale_m = MLA_QK ** -0.5 * LOG2_E
    scale_g = GQA_HEAD_DIM ** -0.5 * LOG2_E
    nope_w = MLA_HEADS * MLA_NOPE

    for r in range(PROJ_SPLIT):
        rows = pl.ds(r * rows_per, rows_per)
        cols = slice(r * rows_per, (r + 1) * rows_per)
        u = _rms(x_ref[rows, :], g1_ref[...]).astype(bf)
        z = _dot(u, win_ref[...])
        c_q = _rms(z[:, Z_CQ:Z_CKV], gq_ref[...]).astype(bf)
        q = _dot(c_q, wq_ref[...])
        c_kv = _rms(z[:, Z_CKV:Z_KPE], gkv_ref[...]).astype(bf)
        k_nope = _dot(c_kv, wkvk_ref[...])
        vt_mla = _dot_t(wkvvt_ref[...], c_kv)
        vt_gqa = _dot_t(winvt_ref[...], u)

        c1, s1 = c1_ref[rows, :], s1_ref[rows, :]
        qm_ref[rows, :nope_w] = (q[:, :nope_w] * scale_m).astype(bf)
        for i in range(MLA_HEADS // 2):
            lo = nope_w + LANES * i
            qm_ref[rows, lo:lo + LANES] = (_rope(q[:, lo:lo + LANES], c1, s1) * scale_m).astype(bf)
        k_rope = _rope(z[:, Z_KPE:Z_GQ], c1, s1)
        km_ref[rows, :nope_w] = k_nope.astype(bf)
        km_ref[rows, nope_w:] = k_rope.astype(bf)
        k_sq = jnp.concatenate([k_nope * k_nope, k_rope * k_rope], axis=1).astype(bf)
        tile_max = jnp.max(_dot(k_sq, sel_ref[...]), axis=0, keepdims=True)
        key_norm2 = tile_max if key_norm2 is None else jnp.maximum(key_norm2, tile_max)
        for h in range(MLA_HEADS):
            vt_ref[VT_ROWS * h:VT_ROWS * h + MLA_V, cols] = vt_mla[MLA_V * h:MLA_V * (h + 1)].astype(bf)

        cg, sg = cg_ref[rows, :], sg_ref[rows, :]
        for h in range(GQA_HEADS):
            t = _rms(z[:, Z_GQ + LANES * h:Z_GQ + LANES * (h + 1)], ggq_ref[...])
            qg_ref[rows, LANES * h:LANES * (h + 1)] = (_rope(t, cg, sg) * scale_g).astype(bf)
        for j in range(GQA_KV_HEADS):
            t = _rms(z[:, Z_GK + LANES * j:Z_GK + LANES * (j + 1)], ggk_ref[...])
            kg_ref[rows, LANES * j:LANES * (j + 1)] = _rope(t, cg, sg).astype(bf)
            lo = VT_ROWS * (MLA_HEADS + j)
            vt_ref[lo:lo + MLA_V, cols] = vt_gqa[LANES * j:LANES * (j + 1)].astype(bf)

    ones = jnp.ones((BF16_ROWS, tm), bf)
    for h in range(N_V_HEADS):
        vt_ref[VT_ROWS * h + MLA_V:VT_ROWS * (h + 1), :] = ones
    kn2_ref[...] = jnp.broadcast_to(key_norm2, (KN_ROWS, LANES))


def _cast_block_spec(rows, cols, n_steps, batch):
    n_blocks = max(d for d in range(1, n_steps + 1) if (rows // BF16_ROWS) % d == 0)
    return pl.BlockSpec((rows // n_blocks, cols),
                        lambda si, b: (jnp.minimum(si * batch + b, n_blocks - 1), 0))


def _projections(x2, g1, win, winvt, gq, wq, gkv, wkvk, wkvvt, ggq, ggk, sel, tables, casts,
                 batch, seq, tm):
    tokens, d_model = x2.shape
    assert (tm // PROJ_SPLIT) % LANES == 0
    ns = seq // tm
    row = lambda si, b: (b * ns + si, 0)
    const = lambda si, b: (0, 0)
    tab = lambda si, b: (si, 0)

    def full(a):
        return pl.BlockSpec(a.shape, const)

    bf = jnp.bfloat16
    cast_specs = [_cast_block_spec(w.shape[0], w.shape[1], ns * batch, batch) for w in casts]
    widths = (MLA_HEADS * MLA_NOPE + LANES * (MLA_HEADS // 2), MLA_HEADS * MLA_NOPE + LANES,
              GQA_HEADS * GQA_HEAD_DIM, GQA_KV_HEADS * GQA_HEAD_DIM)
    vt_rows = N_V_HEADS * VT_ROWS
    return pl.pallas_call(
        _proj_kernel,
        grid=(ns, batch),
        in_specs=[pl.BlockSpec((tm, d_model), row), full(g1), full(win), full(winvt), full(gq),
                  full(wq), full(gkv), full(wkvk), full(wkvvt), full(ggq), full(ggk), full(sel)]
                 + [pl.BlockSpec((tm, LANES), tab) for _ in tables] + cast_specs,
        out_specs=[pl.BlockSpec((tm, w), row) for w in widths]
                  + [pl.BlockSpec((None, vt_rows, tm), lambda si, b: (b, 0, si)),
                     pl.BlockSpec((None, None, KN_ROWS, LANES), lambda si, b: (si, b, 0, 0))]
                  + cast_specs,
        out_shape=[jax.ShapeDtypeStruct((tokens, w), bf) for w in widths]
                  + [jax.ShapeDtypeStruct((batch, vt_rows, seq), bf),
                     jax.ShapeDtypeStruct((ns, batch, KN_ROWS, LANES), jnp.float32)]
                  + [jax.ShapeDtypeStruct(w.shape, bf) for w in casts],
        compiler_params=pltpu.CompilerParams(
            dimension_semantics=("arbitrary", "arbitrary"), vmem_limit_bytes=VMEM_LIMIT),
        name="proj",
    )(x2, g1, win, winvt, gq, wq, gkv, wkvk, wkvvt, ggq, ggk, sel, *tables, *casts)


def _attn_kernel(qm_ref, qg_ref, km_ref, kg_ref, vt_ref, kmax_ref, x_ref, gnm_ref, gng_ref, wo_ref,
                 h_ref, st_ref, p_ref, ot_ref, o_ref):
    tq = qm_ref.shape[0]
    seq = km_ref.shape[0]
    kc = min(KEY_CHUNK, seq)
    n_chunks = seq // kc
    bf = jnp.bfloat16
    nope_w = MLA_HEADS * MLA_NOPE
    lane = lax.broadcasted_iota(jnp.int32, (1, LANES), 1)
    first_of_pair = (lane % (LANES // 2)) < (MLA_ROPE // 2)

    def mla_operands(h, keys):
        own = first_of_pair if h % 2 == 0 else jnp.logical_not(first_of_pair)
        rope_lo = nope_w + LANES * (h // 2)
        q_rope = jnp.where(own, qm_ref[:, rope_lo:rope_lo + LANES], jnp.zeros((), bf))
        q = jnp.concatenate([qm_ref[:, LANES * h:LANES * (h + 1)], q_rope], axis=1)
        k = jnp.concatenate([km_ref[keys, LANES * h:LANES * (h + 1)],
                             km_ref[keys, nope_w:nope_w + LANES]], axis=1)
        return k, q

    def gqa_operands(h, keys):
        j = h // GQA_GROUP
        return kg_ref[keys, LANES * j:LANES * (j + 1)], qg_ref[:, LANES * h:LANES * (h + 1)]

    units = ([(mla_operands, h, VT_ROWS * h) for h in range(MLA_HEADS)]
             + [(gqa_operands, h, VT_ROWS * (MLA_HEADS + h // GQA_GROUP)) for h in range(GQA_HEADS)])
    n_units = len(units)
    min_denom = []

    def weighted_values(u, c):
        vt_lo = units[u][2]
        keys = slice(kc * c, kc * (c + 1))
        pv = _dot(vt_ref[vt_lo:vt_lo + VT_ROWS, keys], p_ref[u % 2, keys, :])
        if c != 0:
            pv += ot_ref[u % 2]
        if c != n_chunks - 1:
            ot_ref[u % 2] = pv
        else:
            denom = pv[MLA_V:MLA_V + 1]
            min_denom.append(denom)
            o_ref[:, LANES * u:LANES * (u + 1)] = (pv[:MLA_V] / denom).T

    def bound_shifted_pipeline():
        shift = [None] * n_units

        def probs(u, c):
            operands, h, vt_lo = units[u]
            k, q = operands(h, slice(kc * c, kc * (c + 1)))
            if c == 0:
                q32 = q.astype(jnp.float32)
                q_norm2 = _dot_t(jnp.ones((BF16_ROWS, q.shape[1]), bf), (q32 * q32).astype(bf))[:1]
                key_head = vt_lo // VT_ROWS
                k_norm = jnp.concatenate([kmax_ref[key_head:key_head + 1, :]] * (tq // LANES), axis=1)
                shift[u] = jnp.broadcast_to(jnp.sqrt(q_norm2) * k_norm, (8, tq))
            s = _dot_t(k, q).reshape(kc // 8, 8, tq)
            p_ref[u % 2, kc * c:kc * (c + 1), :] = jnp.exp2(s - shift[u][None]).reshape(kc, tq).astype(bf)

        for t in range(n_units + 1):
            for c in range(n_chunks):
                if t < n_units:
                    probs(t, c)
                if t >= 1:
                    weighted_values(t - 1, c)
        worst = functools.reduce(jnp.minimum, min_denom)
        del min_denom[:]
        return jnp.min(worst)

    def row_max_pipeline():
        col_max = [None] * n_units

        def scores(u, c):
            operands, h, _ = units[u]
            s = _dot_t(*operands(h, slice(kc * c, kc * (c + 1))))
            st_ref[u % 2, kc * c:kc * (c + 1), :] = s
            cm = jnp.max(s.reshape(kc // 8, 8, tq), axis=0)
            col_max[u] = cm if c == 0 else jnp.maximum(col_max[u], cm)
            if c == n_chunks - 1:
                col_max[u] = jnp.broadcast_to(jnp.max(col_max[u], axis=0, keepdims=True), (8, tq))

        def probs(u, c):
            s = st_ref[u % 2, kc * c:kc * (c + 1), :].reshape(kc // 8, 8, tq)
            p = jnp.exp2(s - col_max[u][None]).reshape(kc, tq)
            p_ref[u % 2, kc * c:kc * (c + 1), :] = p.astype(bf)

        for t in range(n_units + PV_LAG):
            for c in range(n_chunks):
                if t < n_units:
                    scores(t, c)
                if 0 <= t - 1 < n_units:
                    probs(t - 1, c)
                if 0 <= t - PV_LAG < n_units:
                    weighted_values(t - PV_LAG, c)
        del min_denom[:]

    smallest = bound_shifted_pipeline()
    pl.when(jnp.logical_not(smallest >= MIN_DENOM))(row_max_pipeline)

    mla_w = MLA_HEADS * MLA_V
    o = jnp.concatenate([_rms(o_ref[:, :mla_w], gnm_ref[...]),
                         _rms(o_ref[:, mla_w:], gng_ref[...])], axis=-1)
    h_ref[...] = x_ref[...] + _dot(o.astype(bf), wo_ref[...])


def _attention(qm, qg, km, kg, vt, kmax, x2, gnm, gng, wo, batch, seq, tq):
    tokens, d_model = x2.shape
    nq = seq // tq
    row = lambda b, qi: (b * nq + qi, 0)
    per_batch = lambda b, qi: (b, 0)
    const = lambda b, qi: (0, 0)

    def full(a):
        return pl.BlockSpec(a.shape, const)

    return pl.pallas_call(
        _attn_kernel,
        grid=(batch, nq),
        in_specs=[pl.BlockSpec((tq, qm.shape[1]), row), pl.BlockSpec((tq, qg.shape[1]), row),
                  pl.BlockSpec((seq, km.shape[1]), per_batch),
                  pl.BlockSpec((seq, kg.shape[1]), per_batch),
                  pl.BlockSpec((None, vt.shape[1], seq), lambda b, qi: (b, 0, 0)),
                  pl.BlockSpec((None, KN_ROWS, LANES), lambda b, qi: (b, 0, 0)),
                  pl.BlockSpec((tq, d_model), row), full(gnm), full(gng), full(wo)],
        out_specs=pl.BlockSpec((tq, d_model), row),
        out_shape=jax.ShapeDtypeStruct((tokens, d_model), jnp.float32),
        scratch_shapes=[pltpu.VMEM((2, seq, tq), jnp.float32),
                        pltpu.VMEM((2, seq, tq), jnp.bfloat16),
                        pltpu.VMEM((2, VT_ROWS, tq), jnp.float32),
                        pltpu.VMEM((tq, d_model), jnp.float32)],
        compiler_params=pltpu.CompilerParams(
            dimension_semantics=("arbitrary", "arbitrary"), vmem_limit_bytes=VMEM_LIMIT),
        name="attn",
    )(qm, qg, km, kg, vt, kmax, x2, gnm, gng, wo)


def _ffn_kernel(h_ref, hp_ref, hn_ref, g2_ref, wup_ref, cw_ref, cb_ref, wdn_ref, gf_ref,
                o_ref, u_ref, hu_ref, act_ref, *, tiles_per_seq, final_norm):
    tm = h_ref.shape[0]
    bf = jnp.bfloat16
    i = pl.program_id(0)
    si = i % tiles_per_seq
    keep_prev = (si != 0).astype(jnp.float32)
    keep_next = (si != tiles_per_seq - 1).astype(jnp.float32)
    g2 = g2_ref[...]
    u_ref[...] = jnp.concatenate([_rms(hp_ref[...], g2) * keep_prev, _rms(h_ref[...], g2),
                                  _rms(hn_ref[...], g2) * keep_next], axis=0).astype(bf)
    lane_tiles = FF_CHUNK // LANES

    def col(part, c, j=0):
        return part * D_FF + FF_CHUNK * c + LANES * j

    def up(c, part):
        t = _dot(u_ref[...], wup_ref[:, col(part, c):col(part, c) + FF_CHUNK])
        for j in range(lane_tiles):
            hu_ref[c % 2, part, j] = t[:, LANES * j:LANES * (j + 1)]

    def conv(c, part, j):
        lo = col(part, c, j)
        w = cw_ref[:, lo:lo + LANES]
        src = hu_ref.at[c % 2, part, j]
        return (src[pl.ds(HALO - 1, tm), :] * w[0:1] + src[pl.ds(HALO, tm), :] * w[1:2]
                + src[pl.ds(HALO + 1, tm), :] * w[2:3] + cb_ref[:, lo:lo + LANES])

    def act(c, j):
        gate = conv(c, 0, j)
        lo = col(0, c, j)
        act_ref[:, lo:lo + LANES] = (gate * jax.nn.sigmoid(gate) * conv(c, 1, j)).astype(bf)

    assert lane_tiles == 2
    up(0, 0)
    up(0, 1)
    for c in range(N_FF_CHUNKS):
        for half in range(2):
            if c + 1 < N_FF_CHUNKS:
                up(c + 1, half)
            act(c, half)
    out = h_ref[...] + _dot(act_ref[...], wdn_ref[...])
    if final_norm:
        out = _rms(out, gf_ref[...])
    o_ref[...] = out


def _conv_ffn(h2, g2, wup, cw, cb, wdn, gf, seq, tm, final_norm):
    tokens, d_model = h2.shape
    assert (tm + 2 * HALO) % BF16_ROWS == 0
    tiles_per_seq = seq // tm
    n_halo_blocks = tokens // HALO
    per_tile = tm // HALO
    const = lambda i: (0, 0)
    kern = functools.partial(_ffn_kernel, tiles_per_seq=tiles_per_seq, final_norm=final_norm)
    return pl.pallas_call(
        kern,
        grid=(tokens // tm,),
        in_specs=[pl.BlockSpec((tm, d_model), lambda i: (i, 0)),
                  pl.BlockSpec((HALO, d_model), lambda i: (jnp.maximum(i * per_tile - 1, 0), 0)),
                  pl.BlockSpec((HALO, d_model),
                               lambda i: (jnp.minimum((i + 1) * per_tile, n_halo_blocks - 1), 0)),
                  pl.BlockSpec(g2.shape, const),
                  pl.BlockSpec(wup.shape, const, pipeline_mode=pl.Buffered(1)),
                  pl.BlockSpec(cw.shape, const), pl.BlockSpec(cb.shape, const),
                  pl.BlockSpec(wdn.shape, const, pipeline_mode=pl.Buffered(1)),
                  pl.BlockSpec(gf.shape, const)],
        out_specs=pl.BlockSpec((tm, d_model), lambda i: (i, 0)),
        out_shape=jax.ShapeDtypeStruct((tokens, d_model), jnp.float32),
        scratch_shapes=[pltpu.VMEM((tm + 2 * HALO, d_model), jnp.bfloat16),
                        pltpu.VMEM((2, 2, FF_CHUNK // LANES, tm + 2 * HALO, LANES), jnp.float32),
                        pltpu.VMEM((tm, D_FF), jnp.bfloat16)],
        compiler_params=pltpu.CompilerParams(
            dimension_semantics=("arbitrary",), vmem_limit_bytes=VMEM_LIMIT),
        name="convffn",
    )(h2, h2, h2, g2, wup, cw, cb, wdn, gf)


def _rope_np(pos, dim):
    inv = np.float32(ROPE_THETA) ** (-np.arange(0, dim, 2, dtype=np.float32) / np.float32(dim))
    ang = pos.astype(np.float32)[:, None] * inv[None, :]
    return np.cos(ang), np.sin(ang)


def _lane_tables(cos_a, sin_a, cos_b, sin_b):
    c = np.concatenate([cos_a, cos_b, cos_a, cos_b], axis=-1)
    s = np.concatenate([-sin_a, -sin_b, sin_a, sin_b], axis=-1)
    return jnp.asarray(c), jnp.asarray(s)


def _interleave_pairs(w, width):
    lead = w.shape[:-1]
    w = w.reshape(lead + (-1, 2, 2, width // 2))
    return jnp.swapaxes(w, -3, -2).reshape(lead + (-1,))


def _pick_tile(seq, want):
    t = min(seq, want)
    assert seq % t == 0 and t % HALO == 0
    return t


def kernel(x, norm1_g, w_in, mla_q_norm_g, mla_w_q_up, mla_kv_norm_g, mla_w_kv_up, gqa_q_norm_g,
           gqa_k_norm_g, group_norm_mla_g, group_norm_gqa_g, w_out, norm2_g, ffn_w_up, ffn_conv_w,
           ffn_conv_b, ffn_w_down, final_norm_g):
    batch, seq, d_model = x.shape
    depth = w_in.shape[0]
    bf = jnp.bfloat16
    tm_proj = _pick_tile(seq, PROJ_ROWS)
    tq = _pick_tile(seq, ATTN_QUERIES)
    tm_ffn = _pick_tile(seq, FFN_ROWS)

    t = np.arange(seq)
    cos1, sin1 = _rope_np(t, MLA_ROPE)
    cos_r, sin_r = _rope_np(t // GRID_W, GQA_HEAD_DIM // 2)
    cos_c, sin_c = _rope_np(t % GRID_W, GQA_HEAD_DIM // 2)
    tables = _lane_tables(cos1, sin1, cos1, sin1) + _lane_tables(cos_r, sin_r, cos_c, sin_c)

    sel = np.zeros((MLA_HEADS * MLA_NOPE + LANES, LANES), np.float32)
    rope_lane = np.arange(LANES)
    for h in range(MLA_HEADS):
        sel[MLA_NOPE * h:MLA_NOPE * (h + 1), h] = 1.0
        own = ((rope_lane % (LANES // 2)) < (MLA_ROPE // 2)) == (h % 2 == 0)
        sel[MLA_HEADS * MLA_NOPE + rope_lane[own], h] = 1.0
    sel = jnp.asarray(sel, bf)

    row = lambda v: v.reshape(1, -1)
    h2 = x.reshape(batch * seq, d_model)
    for l in range(depth):
        kpe_w = w_in[l][:, OFF_KPE:OFF_GQ]
        win = jnp.concatenate(
            [w_in[l][:, :OFF_KPE],
             _interleave_pairs(jnp.concatenate([kpe_w, kpe_w], axis=1), MLA_ROPE),
             _interleave_pairs(w_in[l][:, OFF_GQ:OFF_GV], GQA_HEAD_DIM // 2)],
            axis=1).astype(bf)
        winvt = w_in[l][:, OFF_GV:IN_WIDTH].T.astype(bf)
        wq3 = mla_w_q_up[l].reshape(MLA_Q_RANK, MLA_HEADS, MLA_QK)
        wq = jnp.concatenate(
            [wq3[:, :, :MLA_NOPE].reshape(MLA_Q_RANK, -1),
             _interleave_pairs(wq3[:, :, MLA_NOPE:].reshape(MLA_Q_RANK, -1), MLA_ROPE)],
            axis=1).astype(bf)
        wkv3 = mla_w_kv_up[l].reshape(MLA_KV_RANK, MLA_HEADS, MLA_NOPE + MLA_V)
        wkvk = wkv3[:, :, :MLA_NOPE].reshape(MLA_KV_RANK, -1).astype(bf)
        wkvvt = wkv3[:, :, MLA_NOPE:].reshape(MLA_KV_RANK, -1).T.astype(bf)
        qm, km, qg, kg, vt, kn2, wo, wup, wdn = _projections(
            h2, row(norm1_g[l]), win, winvt, row(mla_q_norm_g[l]), wq, row(mla_kv_norm_g[l]),
            wkvk, wkvvt, row(_interleave_pairs(gqa_q_norm_g[l], GQA_HEAD_DIM // 2)),
            row(_interleave_pairs(gqa_k_norm_g[l], GQA_HEAD_DIM // 2)), sel, tables,
            (w_out[l], ffn_w_up[l], ffn_w_down[l]), batch, seq, tm_proj)
        k_mla = jnp.sqrt(jnp.max(kn2[:, :, 0, :MLA_HEADS], axis=0))
        k_gqa = jnp.full((batch, GQA_KV_HEADS),
                         GQA_HEAD_DIM ** 0.5) * jnp.max(jnp.abs(gqa_k_norm_g[l]))
        kmax = jnp.concatenate(
            [k_mla, k_gqa, jnp.zeros((batch, KN_ROWS - N_V_HEADS), jnp.float32)], axis=1)
        kmax = jnp.broadcast_to(kmax[:, :, None], (batch, KN_ROWS, LANES))
        h2 = _attention(qm, qg, km, kg, vt, kmax, h2, row(group_norm_mla_g[l]),
                        row(group_norm_gqa_g[l]), wo, batch, seq, tq)
        h2 = _conv_ffn(h2, row(norm2_g[l]), wup, ffn_conv_w[l], row(ffn_conv_b[l]), wdn,
                       row(final_norm_g), seq, tm_ffn, final_norm=(l == depth - 1))
    return h2.reshape(batch, seq, d_model)
```

```python
import functools

import numpy as np

import jax
import jax.numpy as jnp
from jax import lax
from jax.experimental import pallas as pl
from jax.experimental.pallas import tpu as pltpu

EPS = 1e-6
LOG2_E = 1.4426950408889634
ROPE_THETA = 10000.0
GRID_W = 64

D_MODEL = 1024
MLA_HEADS = 4
MLA_Q_RANK = 384
MLA_KV_RANK = 256
MLA_NOPE = 128
MLA_ROPE = 64
MLA_V = 128
MLA_QK = MLA_NOPE + MLA_ROPE

GQA_HEADS = 4
GQA_KV_HEADS = 2
GQA_HEAD_DIM = 128
GQA_GROUP = GQA_HEADS // GQA_KV_HEADS

D_FF = 2816
LANES = 128
BF16_ROWS = 16

OFF_KPE = MLA_Q_RANK + MLA_KV_RANK
OFF_GQ = OFF_KPE + MLA_ROPE
OFF_GK = OFF_GQ + GQA_HEADS * GQA_HEAD_DIM
OFF_GV = OFF_GK + GQA_KV_HEADS * GQA_HEAD_DIM
IN_WIDTH = OFF_GV + GQA_KV_HEADS * GQA_HEAD_DIM

Z_CQ = 0
Z_CKV = Z_CQ + MLA_Q_RANK
Z_KPE = Z_CKV + MLA_KV_RANK
Z_GQ = Z_KPE + LANES
Z_GK = Z_GQ + GQA_HEADS * GQA_HEAD_DIM
Z_WIDTH = Z_GK + GQA_KV_HEADS * GQA_HEAD_DIM

N_V_HEADS = MLA_HEADS + GQA_KV_HEADS
VT_ROWS = MLA_V + BF16_ROWS
KN_ROWS = 8
MIN_DENOM = 2.0 ** -60
MAX_DENOM = 2.0 ** 40

FF_CHUNK = 256
N_FF_CHUNKS = D_FF // FF_CHUNK
HALO = 8
KEY_CHUNK = 1024
PV_LAG = 2
PROJ_SPLIT = 4

PROJ_ROWS = 1024
ATTN_QUERIES = 512
FFN_ROWS = 512

V7X_VMEM_BYTES = 64 * 1024 * 1024
VMEM_LIMIT = V7X_VMEM_BYTES * 7 // 8


def _rms(x, g):
    y = x * lax.rsqrt(jnp.mean(x * x, axis=-1, keepdims=True) + EPS)
    return y * g


def _rope(x, c, s):
    return x * c + pltpu.roll(x, LANES // 2, 1) * s


def _dot(a, b):
    return jnp.dot(a, b, preferred_element_type=jnp.float32)


def _dot_t(a, b):
    return lax.dot_general(a, b, (((1,), (1,)), ((), ())), preferred_element_type=jnp.float32)


def _proj_kernel(x_ref, g1_ref, win_ref, winvt_ref, gq_ref, wq_ref, gkv_ref, wkvk_ref, wkvvt_ref,
                 ggq_ref, ggk_ref, sel_ref, c1_ref, s1_ref, cg_ref, sg_ref, *refs):
    n_casts = (len(refs) - 6) // 2
    qm_ref, km_ref, qg_ref, kg_ref, vt_ref, kn2_ref = refs[n_casts:n_casts + 6]
    bf = jnp.bfloat16
    for src, dst in zip(refs[:n_casts], refs[n_casts + 6:]):
        dst[...] = src[...].astype(bf)
    tm = x_ref.shape[0]
    rows_per = tm // PROJ_SPLIT
    scale_m = MLA_QK ** -0.5 * LOG2_E
    scale_g = GQA_HEAD_DIM ** -0.5 * LOG2_E
    nope_w = MLA_HEADS * MLA_NOPE

    for r in range(PROJ_SPLIT):
        rows = pl.ds(r * rows_per, rows_per)
        cols = slice(r * rows_per, (r + 1) * rows_per)
        u = _rms(x_ref[rows, :], g1_ref[...]).astype(bf)
        z = _dot(u, win_ref[...])
        c_q = _rms(z[:, Z_CQ:Z_CKV], gq_ref[...]).astype(bf)
        q = _dot(c_q, wq_ref[...])
        c_kv = _rms(z[:, Z_CKV:Z_KPE], gkv_ref[...]).astype(bf)
        k_nope = _dot(c_kv, wkvk_ref[...])
        vt_mla = _dot_t(wkvvt_ref[...], c_kv)
        vt_gqa = _dot_t(winvt_ref[...], u)

        c1, s1 = c1_ref[rows, :], s1_ref[rows, :]
        qm_ref[rows, :nope_w] = (q[:, :nope_w] * scale_m).astype(bf)
        for i in range(MLA_HEADS // 2):
            lo = nope_w + LANES * i
            qm_ref[rows, lo:lo + LANES] = (_rope(q[:, lo:lo + LANES], c1, s1) * scale_m).astype(bf)
        k_rope = _rope(z[:, Z_KPE:Z_GQ], c1, s1)
        km_ref[rows, :nope_w] = k_nope.astype(bf)
        km_ref[rows, nope_w:] = k_rope.astype(bf)
        if r == 0:
            k_sq = jnp.concatenate([k_nope * k_nope, k_rope * k_rope], axis=1).astype(bf)
            key_norm2 = jnp.max(_dot(k_sq, sel_ref[...]), axis=0, keepdims=True)
        for h in range(MLA_HEADS):
            vt_ref[VT_ROWS * h:VT_ROWS * h + MLA_V, cols] = vt_mla[MLA_V * h:MLA_V * (h + 1)].astype(bf)

        cg, sg = cg_ref[rows, :], sg_ref[rows, :]
        for h in range(GQA_HEADS):
            t = _rms(z[:, Z_GQ + LANES * h:Z_GQ + LANES * (h + 1)], ggq_ref[...])
            qg_ref[rows, LANES * h:LANES * (h + 1)] = (_rope(t, cg, sg) * scale_g).astype(bf)
        for j in range(GQA_KV_HEADS):
            t = _rms(z[:, Z_GK + LANES * j:Z_GK + LANES * (j + 1)], ggk_ref[...])
            kg_ref[rows, LANES * j:LANES * (j + 1)] = _rope(t, cg, sg).astype(bf)
            lo = VT_ROWS * (MLA_HEADS + j)
            vt_ref[lo:lo + MLA_V, cols] = vt_gqa[LANES * j:LANES * (j + 1)].astype(bf)

    ones = jnp.ones((BF16_ROWS, tm), bf)
    for h in range(N_V_HEADS):
        vt_ref[VT_ROWS * h + MLA_V:VT_ROWS * (h + 1), :] = ones
    kn2_ref[...] = jnp.broadcast_to(key_norm2, (KN_ROWS, LANES))


def _cast_block_spec(rows, cols, n_steps, batch):
    n_blocks = max(d for d in range(1, n_steps + 1) if (rows // BF16_ROWS) % d == 0)
    return pl.BlockSpec((rows // n_blocks, cols),
                        lambda si, b: (jnp.minimum(si * batch + b, n_blocks - 1), 0))


def _projections(x2, g1, win, winvt, gq, wq, gkv, wkvk, wkvvt, ggq, ggk, sel, tables, casts,
                 batch, seq, tm):
    tokens, d_model = x2.shape
    assert (tm // PROJ_SPLIT) % LANES == 0
    ns = seq // tm
    row = lambda si, b: (b * ns + si, 0)
    const = lambda si, b: (0, 0)
    tab = lambda si, b: (si, 0)

    def full(a):
        return pl.BlockSpec(a.shape, const)

    bf = jnp.bfloat16
    cast_specs = [_cast_block_spec(w.shape[0], w.shape[1], ns * batch, batch) for w in casts]
    widths = (MLA_HEADS * MLA_NOPE + LANES * (MLA_HEADS // 2), MLA_HEADS * MLA_NOPE + LANES,
              GQA_HEADS * GQA_HEAD_DIM, GQA_KV_HEADS * GQA_HEAD_DIM)
    vt_rows = N_V_HEADS * VT_ROWS
    return pl.pallas_call(
        _proj_kernel,
        grid=(ns, batch),
        in_specs=[pl.BlockSpec((tm, d_model), row), full(g1), full(win), full(winvt), full(gq),
                  full(wq), full(gkv), full(wkvk), full(wkvvt), full(ggq), full(ggk), full(sel)]
                 + [pl.BlockSpec((tm, LANES), tab) for _ in tables] + cast_specs,
        out_specs=[pl.BlockSpec((tm, w), row) for w in widths]
                  + [pl.BlockSpec((None, vt_rows, tm), lambda si, b: (b, 0, si)),
                     pl.BlockSpec((None, None, KN_ROWS, LANES), lambda si, b: (si, b, 0, 0))]
                  + cast_specs,
        out_shape=[jax.ShapeDtypeStruct((tokens, w), bf) for w in widths]
                  + [jax.ShapeDtypeStruct((batch, vt_rows, seq), bf),
                     jax.ShapeDtypeStruct((ns, batch, KN_ROWS, LANES), jnp.float32)]
                  + [jax.ShapeDtypeStruct(w.shape, bf) for w in casts],
        compiler_params=pltpu.CompilerParams(
            dimension_semantics=("arbitrary", "arbitrary"), vmem_limit_bytes=VMEM_LIMIT),
        name="proj",
    )(x2, g1, win, winvt, gq, wq, gkv, wkvk, wkvvt, ggq, ggk, sel, *tables, *casts)


def _attn_kernel(qm_ref, qg_ref, km_ref, kg_ref, vt_ref, kmax_ref, x_ref, gnm_ref, gng_ref, wo_ref,
                 h_ref, st_ref, p_ref, ot_ref, o_ref):
    tq = qm_ref.shape[0]
    seq = km_ref.shape[0]
    kc = min(KEY_CHUNK, seq)
    n_chunks = seq // kc
    bf = jnp.bfloat16
    nope_w = MLA_HEADS * MLA_NOPE
    lane = lax.broadcasted_iota(jnp.int32, (1, LANES), 1)
    first_of_pair = (lane % (LANES // 2)) < (MLA_ROPE // 2)

    def mla_operands(h, keys):
        own = first_of_pair if h % 2 == 0 else jnp.logical_not(first_of_pair)
        rope_lo = nope_w + LANES * (h // 2)
        q_rope = jnp.where(own, qm_ref[:, rope_lo:rope_lo + LANES], jnp.zeros((), bf))
        q = jnp.concatenate([qm_ref[:, LANES * h:LANES * (h + 1)], q_rope], axis=1)
        k = jnp.concatenate([km_ref[keys, LANES * h:LANES * (h + 1)],
                             km_ref[keys, nope_w:nope_w + LANES]], axis=1)
        return k, q

    def gqa_operands(h, keys):
        j = h // GQA_GROUP
        return kg_ref[keys, LANES * j:LANES * (j + 1)], qg_ref[:, LANES * h:LANES * (h + 1)]

    units = ([(mla_operands, h, VT_ROWS * h) for h in range(MLA_HEADS)]
             + [(gqa_operands, h, VT_ROWS * (MLA_HEADS + h // GQA_GROUP)) for h in range(GQA_HEADS)])
    n_units = len(units)
    min_denom = []

    def weighted_values(u, c):
        vt_lo = units[u][2]
        keys = slice(kc * c, kc * (c + 1))
        pv = _dot(vt_ref[vt_lo:vt_lo + VT_ROWS, keys], p_ref[u % 2, keys, :])
        if c != 0:
            pv += ot_ref[u % 2]
        if c != n_chunks - 1:
            ot_ref[u % 2] = pv
        else:
            denom = pv[MLA_V:MLA_V + 1]
            min_denom.append(denom)
            o_ref[:, LANES * u:LANES * (u + 1)] = (pv[:MLA_V] / denom).T

    def bound_shifted_pipeline():
        shift = [None] * n_units

        def probs(u, c):
            operands, h, vt_lo = units[u]
            k, q = operands(h, slice(kc * c, kc * (c + 1)))
            if c == 0:
                q32 = q.astype(jnp.float32)
                q_norm2 = _dot_t(jnp.ones((BF16_ROWS, q.shape[1]), bf), (q32 * q32).astype(bf))[:1]
                key_head = vt_lo // VT_ROWS
                k_norm = jnp.concatenate([kmax_ref[key_head:key_head + 1, :]] * (tq // LANES), axis=1)
                shift[u] = jnp.broadcast_to(jnp.sqrt(q_norm2) * k_norm, (8, tq))
            s = _dot_t(k, q).reshape(kc // 8, 8, tq)
            p_ref[u % 2, kc * c:kc * (c + 1), :] = jnp.exp2(s - shift[u][None]).reshape(kc, tq).astype(bf)

        for t in range(n_units + 1):
            for c in range(n_chunks):
                if t < n_units:
                    probs(t, c)
                if t >= 1:
                    weighted_values(t - 1, c)
        smallest = jnp.min(functools.reduce(jnp.minimum, min_denom))
        largest = jnp.max(functools.reduce(jnp.maximum, min_denom))
        del min_denom[:]
        return smallest, largest

    def row_max_pipeline():
        col_max = [None] * n_units

        def scores(u, c):
            operands, h, _ = units[u]
            s = _dot_t(*operands(h, slice(kc * c, kc * (c + 1))))
            st_ref[u % 2, kc * c:kc * (c + 1), :] = s
            cm = jnp.max(s.reshape(kc // 8, 8, tq), axis=0)
            col_max[u] = cm if c == 0 else jnp.maximum(col_max[u], cm)
            if c == n_chunks - 1:
                col_max[u] = jnp.broadcast_to(jnp.max(col_max[u], axis=0, keepdims=True), (8, tq))

        def probs(u, c):
            s = st_ref[u % 2, kc * c:kc * (c + 1), :].reshape(kc // 8, 8, tq)
            p = jnp.exp2(s - col_max[u][None]).reshape(kc, tq)
            p_ref[u % 2, kc * c:kc * (c + 1), :] = p.astype(bf)

        for t in range(n_units + PV_LAG):
            for c in range(n_chunks):
                if t < n_units:
                    scores(t, c)
                if 0 <= t - 1 < n_units:
                    probs(t - 1, c)
                if 0 <= t - PV_LAG < n_units:
                    weighted_values(t - PV_LAG, c)
        del min_denom[:]

    smallest, largest = bound_shifted_pipeline()
    in_range = jnp.logical_and(smallest >= MIN_DENOM, largest <= MAX_DENOM)
    pl.when(jnp.logical_not(in_range))(row_max_pipeline)

    mla_w = MLA_HEADS * MLA_V
    o = jnp.concatenate([_rms(o_ref[:, :mla_w], gnm_ref[...]),
                         _rms(o_ref[:, mla_w:], gng_ref[...])], axis=-1)
    h_ref[...] = x_ref[...] + _dot(o.astype(bf), wo_ref[...])


def _attention(qm, qg, km, kg, vt, kmax, x2, gnm, gng, wo, batch, seq, tq):
    tokens, d_model = x2.shape
    nq = seq // tq
    row = lambda b, qi: (b * nq + qi, 0)
    per_batch = lambda b, qi: (b, 0)
    const = lambda b, qi: (0, 0)

    def full(a):
        return pl.BlockSpec(a.shape, const)

    return pl.pallas_call(
        _attn_kernel,
        grid=(batch, nq),
        in_specs=[pl.BlockSpec((tq, qm.shape[1]), row), pl.BlockSpec((tq, qg.shape[1]), row),
                  pl.BlockSpec((seq, km.shape[1]), per_batch),
                  pl.BlockSpec((seq, kg.shape[1]), per_batch),
                  pl.BlockSpec((None, vt.shape[1], seq), lambda b, qi: (b, 0, 0)),
                  pl.BlockSpec((None, KN_ROWS, LANES), lambda b, qi: (b, 0, 0)),
                  pl.BlockSpec((tq, d_model), row), full(gnm), full(gng), full(wo)],
        out_specs=pl.BlockSpec((tq, d_model), row),
        out_shape=jax.ShapeDtypeStruct((tokens, d_model), jnp.float32),
        scratch_shapes=[pltpu.VMEM((2, seq, tq), jnp.float32),
                        pltpu.VMEM((2, seq, tq), jnp.bfloat16),
                        pltpu.VMEM((2, VT_ROWS, tq), jnp.float32),
                        pltpu.VMEM((tq, d_model), jnp.float32)],
        compiler_params=pltpu.CompilerParams(
            dimension_semantics=("arbitrary", "arbitrary"), vmem_limit_bytes=VMEM_LIMIT),
        name="attn",
    )(qm, qg, km, kg, vt, kmax, x2, gnm, gng, wo)


def _ffn_kernel(h_ref, hp_ref, hn_ref, g2_ref, wup_ref, cw_ref, cb_ref, wdn_ref, gf_ref,
                o_ref, u_ref, hu_ref, act_ref, *, tiles_per_seq, final_norm):
    tm = h_ref.shape[0]
    bf = jnp.bfloat16
    i = pl.program_id(0)
    si = i % tiles_per_seq
    keep_prev = (si != 0).astype(jnp.float32)
    keep_next = (si != tiles_per_seq - 1).astype(jnp.float32)
    g2 = g2_ref[...]
    u_ref[...] = jnp.concatenate([_rms(hp_ref[...], g2) * keep_prev, _rms(h_ref[...], g2),
                                  _rms(hn_ref[...], g2) * keep_next], axis=0).astype(bf)
    lane_tiles = FF_CHUNK // LANES

    def col(part, c, j=0):
        return part * D_FF + FF_CHUNK * c + LANES * j

    def up(c, part):
        t = _dot(u_ref[...], wup_ref[:, col(part, c):col(part, c) + FF_CHUNK])
        for j in range(lane_tiles):
            hu_ref[c % 2, part, j] = t[:, LANES * j:LANES * (j + 1)]

    def conv(c, part, j):
        lo = col(part, c, j)
        w = cw_ref[:, lo:lo + LANES]
        src = hu_ref.at[c % 2, part, j]
        return (src[pl.ds(HALO - 1, tm), :] * w[0:1] + src[pl.ds(HALO, tm), :] * w[1:2]
                + src[pl.ds(HALO + 1, tm), :] * w[2:3] + cb_ref[:, lo:lo + LANES])

    def act(c, j):
        gate = conv(c, 0, j)
        lo = col(0, c, j)
        act_ref[:, lo:lo + LANES] = (gate * jax.nn.sigmoid(gate) * conv(c, 1, j)).astype(bf)

    assert lane_tiles == 2
    up(0, 0)
    up(0, 1)
    for c in range(N_FF_CHUNKS):
        for half in range(2):
            if c + 1 < N_FF_CHUNKS:
                up(c + 1, half)
            act(c, half)
    out = h_ref[...] + _dot(act_ref[...], wdn_ref[...])
    if final_norm:
        out = _rms(out, gf_ref[...])
    o_ref[...] = out


def _conv_ffn(h2, g2, wup, cw, cb, wdn, gf, seq, tm, final_norm):
    tokens, d_model = h2.shape
    assert (tm + 2 * HALO) % BF16_ROWS == 0
    tiles_per_seq = seq // tm
    n_halo_blocks = tokens // HALO
    per_tile = tm // HALO
    const = lambda i: (0, 0)
    kern = functools.partial(_ffn_kernel, tiles_per_seq=tiles_per_seq, final_norm=final_norm)
    return pl.pallas_call(
        kern,
        grid=(tokens // tm,),
        in_specs=[pl.BlockSpec((tm, d_model), lambda i: (i, 0)),
                  pl.BlockSpec((HALO, d_model), lambda i: (jnp.maximum(i * per_tile - 1, 0), 0)),
                  pl.BlockSpec((HALO, d_model),
                               lambda i: (jnp.minimum((i + 1) * per_tile, n_halo_blocks - 1), 0)),
                  pl.BlockSpec(g2.shape, const),
                  pl.BlockSpec(wup.shape, const, pipeline_mode=pl.Buffered(1)),
                  pl.BlockSpec(cw.shape, const), pl.BlockSpec(cb.shape, const),
                  pl.BlockSpec(wdn.shape, const, pipeline_mode=pl.Buffered(1)),
                  pl.BlockSpec(gf.shape, const)],
        out_specs=pl.BlockSpec((tm, d_model), lambda i: (i, 0)),
        out_shape=jax.ShapeDtypeStruct((tokens, d_model), jnp.float32),
        scratch_shapes=[pltpu.VMEM((tm + 2 * HALO, d_model), jnp.bfloat16),
                        pltpu.VMEM((2, 2, FF_CHUNK // LANES, tm + 2 * HALO, LANES), jnp.float32),
                        pltpu.VMEM((tm, D_FF), jnp.bfloat16)],
        compiler_params=pltpu.CompilerParams(
            dimension_semantics=("arbitrary",), vmem_limit_bytes=VMEM_LIMIT),
        name="convffn",
    )(h2, h2, h2, g2, wup, cw, cb, wdn, gf)


def _rope_np(pos, dim):
    inv = np.float32(ROPE_THETA) ** (-np.arange(0, dim, 2, dtype=np.float32) / np.float32(dim))
    ang = pos.astype(np.float32)[:, None] * inv[None, :]
    return np.cos(ang), np.sin(ang)


def _lane_tables(cos_a, sin_a, cos_b, sin_b):
    c = np.concatenate([cos_a, cos_b, cos_a, cos_b], axis=-1)
    s = np.concatenate([-sin_a, -sin_b, sin_a, sin_b], axis=-1)
    return jnp.asarray(c), jnp.asarray(s)


def _interleave_pairs(w, width):
    lead = w.shape[:-1]
    w = w.reshape(lead + (-1, 2, 2, width // 2))
    return jnp.swapaxes(w, -3, -2).reshape(lead + (-1,))


def _pick_tile(seq, want):
    t = min(seq, want)
    assert seq % t == 0 and t % HALO == 0
    return t


def kernel(x, norm1_g, w_in, mla_q_norm_g, mla_w_q_up, mla_kv_norm_g, mla_w_kv_up, gqa_q_norm_g,
           gqa_k_norm_g, group_norm_mla_g, group_norm_gqa_g, w_out, norm2_g, ffn_w_up, ffn_conv_w,
           ffn_conv_b, ffn_w_down, final_norm_g):
    batch, seq, d_model = x.shape
    depth = w_in.shape[0]
    bf = jnp.bfloat16
    tm_proj = _pick_tile(seq, PROJ_ROWS)
    tq = _pick_tile(seq, ATTN_QUERIES)
    tm_ffn = _pick_tile(seq, FFN_ROWS)

    t = np.arange(seq)
    cos1, sin1 = _rope_np(t, MLA_ROPE)
    cos_r, sin_r = _rope_np(t // GRID_W, GQA_HEAD_DIM // 2)
    cos_c, sin_c = _rope_np(t % GRID_W, GQA_HEAD_DIM // 2)
    tables = _lane_tables(cos1, sin1, cos1, sin1) + _lane_tables(cos_r, sin_r, cos_c, sin_c)

    sel = np.zeros((MLA_HEADS * MLA_NOPE + LANES, LANES), np.float32)
    rope_lane = np.arange(LANES)
    for h in range(MLA_HEADS):
        sel[MLA_NOPE * h:MLA_NOPE * (h + 1), h] = 1.0
        own = ((rope_lane % (LANES // 2)) < (MLA_ROPE // 2)) == (h % 2 == 0)
        sel[MLA_HEADS * MLA_NOPE + rope_lane[own], h] = 1.0
    sel = jnp.asarray(sel, bf)

    row = lambda v: v.reshape(1, -1)
    h2 = x.reshape(batch * seq, d_model)
    for l in range(depth):
        kpe_w = w_in[l][:, OFF_KPE:OFF_GQ]
        win = jnp.concatenate(
            [w_in[l][:, :OFF_KPE],
             _interleave_pairs(jnp.concatenate([kpe_w, kpe_w], axis=1), MLA_ROPE),
             _interleave_pairs(w_in[l][:, OFF_GQ:OFF_GV], GQA_HEAD_DIM // 2)],
            axis=1).astype(bf)
        winvt = w_in[l][:, OFF_GV:IN_WIDTH].T.astype(bf)
        wq3 = mla_w_q_up[l].reshape(MLA_Q_RANK, MLA_HEADS, MLA_QK)
        wq = jnp.concatenate(
            [wq3[:, :, :MLA_NOPE].reshape(MLA_Q_RANK, -1),
             _interleave_pairs(wq3[:, :, MLA_NOPE:].reshape(MLA_Q_RANK, -1), MLA_ROPE)],
            axis=1).astype(bf)
        wkv3 = mla_w_kv_up[l].reshape(MLA_KV_RANK, MLA_HEADS, MLA_NOPE + MLA_V)
        wkvk = wkv3[:, :, :MLA_NOPE].reshape(MLA_KV_RANK, -1).astype(bf)
        wkvvt = wkv3[:, :, MLA_NOPE:].reshape(MLA_KV_RANK, -1).T.astype(bf)
        qm, km, qg, kg, vt, kn2, wo, wup, wdn = _projections(
            h2, row(norm1_g[l]), win, winvt, row(mla_q_norm_g[l]), wq, row(mla_kv_norm_g[l]),
            wkvk, wkvvt, row(_interleave_pairs(gqa_q_norm_g[l], GQA_HEAD_DIM // 2)),
            row(_interleave_pairs(gqa_k_norm_g[l], GQA_HEAD_DIM // 2)), sel, tables,
            (w_out[l], ffn_w_up[l], ffn_w_down[l]), batch, seq, tm_proj)
        k_mla = jnp.sqrt(jnp.max(kn2[:, :, 0, :MLA_HEADS], axis=0))
        k_gqa = jnp.full((batch, GQA_KV_HEADS),
                         GQA_HEAD_DIM ** 0.5) * jnp.max(jnp.abs(gqa_k_norm_g[l]))
        kmax = jnp.concatenate(
            [k_mla, k_gqa, jnp.zeros((batch, KN_ROWS - N_V_HEADS), jnp.float32)], axis=1)
        kmax = jnp.broadcast_to(kmax[:, :, None], (batch, KN_ROWS, LANES))
        h2 = _attention(qm, qg, km, kg, vt, kmax, h2, row(group_norm_mla_g[l]),
                        row(group_norm_gqa_g[l]), wo, batch, seq, tq)
        h2 = _conv_ffn(h2, row(norm2_g[l]), wup, ffn_conv_w[l], row(ffn_conv_b[l]), wdn,
                       row(final_norm_g), seq, tm_ffn, final_norm=(l == depth - 1))
    return h2.reshape(batch, seq, d_model)
```

```python
import functools

import numpy as np

import jax
import jax.numpy as jnp
from jax import lax
from jax.experimental import pallas as pl
from jax.experimental.pallas import tpu as pltpu

EPS = 1e-6
LOG2_E = 1.4426950408889634
ROPE_THETA = 10000.0
GRID_W = 64

D_MODEL = 1024
MLA_HEADS = 4
MLA_Q_RANK = 384
MLA_KV_RANK = 256
MLA_NOPE = 128
MLA_ROPE = 64
MLA_V = 128
MLA_QK = MLA_NOPE + MLA_ROPE

GQA_HEADS = 4
GQA_KV_HEADS = 2
GQA_HEAD_DIM = 128
GQA_GROUP = GQA_HEADS // GQA_KV_HEADS

D_FF = 2816
LANES = 128
BF16_ROWS = 16

OFF_KPE = MLA_Q_RANK + MLA_KV_RANK
OFF_GQ = OFF_KPE + MLA_ROPE
OFF_GK = OFF_GQ + GQA_HEADS * GQA_HEAD_DIM
OFF_GV = OFF_GK + GQA_KV_HEADS * GQA_HEAD_DIM
IN_WIDTH = OFF_GV + GQA_KV_HEADS * GQA_HEAD_DIM

Z_CQ = 0
Z_CKV = Z_CQ + MLA_Q_RANK
Z_KPE = Z_CKV + MLA_KV_RANK
Z_GQ = Z_KPE + LANES
Z_GK = Z_GQ + GQA_HEADS * GQA_HEAD_DIM
Z_WIDTH = Z_GK + GQA_KV_HEADS * GQA_HEAD_DIM

N_V_HEADS = MLA_HEADS + GQA_KV_HEADS
VT_ROWS = MLA_V + BF16_ROWS
KN_ROWS = 8
MIN_DENOM = 2.0 ** -60
MAX_DENOM = 2.0 ** 40

FF_CHUNK = 256
N_FF_CHUNKS = D_FF // FF_CHUNK
HALO = 8
KEY_CHUNK = 1024
PV_LAG = 2
PROJ_SPLIT = 4

PROJ_ROWS = 1024
ATTN_QUERIES = 512
FFN_ROWS = 512

V7X_VMEM_BYTES = 64 * 1024 * 1024
VMEM_LIMIT = V7X_VMEM_BYTES * 7 // 8


def _rms(x, g):
    y = x * lax.rsqrt(jnp.mean(x * x, axis=-1, keepdims=True) + EPS)
    return y * g


def _rope(x, c, s):
    return x * c + pltpu.roll(x, LANES // 2, 1) * s


def _dot(a, b):
    return jnp.dot(a, b, preferred_element_type=jnp.float32)


def _dot_t(a, b):
    return lax.dot_general(a, b, (((1,), (1,)), ((), ())), preferred_element_type=jnp.float32)


def _proj_kernel(x_ref, g1_ref, win_ref, winvt_ref, gq_ref, wq_ref, gkv_ref, wkvk_ref, wkvvt_ref,
                 ggq_ref, ggk_ref, sel_ref, c1_ref, s1_ref, cg_ref, sg_ref, *refs):
    n_casts = (len(refs) - 6) // 2
    qm_ref, km_ref, qg_ref, kg_ref, vt_ref, kn2_ref = refs[n_casts:n_casts + 6]
    bf = jnp.bfloat16
    for src, dst in zip(refs[:n_casts], refs[n_casts + 6:]):
        dst[...] = src[...].astype(bf)
    tm = x_ref.shape[0]
    rows_per = tm // PROJ_SPLIT
    scale_m = MLA_QK ** -0.5 * LOG2_E
    scale_g = GQA_HEAD_DIM ** -0.5 * LOG2_E
    nope_w = MLA_HEADS * MLA_NOPE

    for r in range(PROJ_SPLIT):
        rows = pl.ds(r * rows_per, rows_per)
        cols = slice(r * rows_per, (r + 1) * rows_per)
        u = _rms(x_ref[rows, :], g1_ref[...]).astype(bf)
        z = _dot(u, win_ref[...])
        c_q = _rms(z[:, Z_CQ:Z_CKV], gq_ref[...]).astype(bf)
        q = _dot(c_q, wq_ref[...])
        c_kv = _rms(z[:, Z_CKV:Z_KPE], gkv_ref[...]).astype(bf)
        k_nope = _dot(c_kv, wkvk_ref[...])
        vt_mla = _dot_t(wkvvt_ref[...], c_kv)
        vt_gqa = _dot_t(winvt_ref[...], u)

        c1, s1 = c1_ref[rows, :], s1_ref[rows, :]
        qm_ref[rows, :nope_w] = (q[:, :nope_w] * scale_m).astype(bf)
        for i in range(MLA_HEADS // 2):
            lo = nope_w + LANES * i
            qm_ref[rows, lo:lo + LANES] = (_rope(q[:, lo:lo + LANES], c1, s1) * scale_m).astype(bf)
        k_rope = _rope(z[:, Z_KPE:Z_GQ], c1, s1)
        km_ref[rows, :nope_w] = k_nope.astype(bf)
        km_ref[rows, nope_w:] = k_rope.astype(bf)
        if r == 0:
            k_sq = jnp.concatenate([k_nope * k_nope, k_rope * k_rope], axis=1).astype(bf)
            key_norm2 = jnp.max(_dot(k_sq, sel_ref[...]), axis=0, keepdims=True)
        for h in range(MLA_HEADS):
            vt_ref[VT_ROWS * h:VT_ROWS * h + MLA_V, cols] = vt_mla[MLA_V * h:MLA_V * (h + 1)].astype(bf)

        cg, sg = cg_ref[rows, :], sg_ref[rows, :]
        for h in range(GQA_HEADS):
            t = _rms(z[:, Z_GQ + LANES * h:Z_GQ + LANES * (h + 1)], ggq_ref[...])
            qg_ref[rows, LANES * h:LANES * (h + 1)] = (_rope(t, cg, sg) * scale_g).astype(bf)
        for j in range(GQA_KV_HEADS):
            t = _rms(z[:, Z_GK + LANES * j:Z_GK + LANES * (j + 1)], ggk_ref[...])
            kg_ref[rows, LANES * j:LANES * (j + 1)] = _rope(t, cg, sg).astype(bf)
            lo = VT_ROWS * (MLA_HEADS + j)
            vt_ref[lo:lo + MLA_V, cols] = vt_gqa[LANES * j:LANES * (j + 1)].astype(bf)

    ones = jnp.ones((BF16_ROWS, tm), bf)
    for h in range(N_V_HEADS):
        vt_ref[VT_ROWS * h + MLA_V:VT_ROWS * (h + 1), :] = ones
    kn2_ref[...] = jnp.broadcast_to(key_norm2, (KN_ROWS, LANES))


def _cast_block_spec(rows, cols, n_steps, batch):
    n_blocks = max(d for d in range(1, n_steps + 1) if (rows // BF16_ROWS) % d == 0)
    return pl.BlockSpec((rows // n_blocks, cols),
                        lambda si, b: (jnp.minimum(si * batch + b, n_blocks - 1), 0))


def _projections(x2, g1, win, winvt, gq, wq, gkv, wkvk, wkvvt, ggq, ggk, sel, tables, casts,
                 batch, seq, tm):
    tokens, d_model = x2.shape
    assert (tm // PROJ_SPLIT) % LANES == 0
    ns = seq // tm
    row = lambda si, b: (b * ns + si, 0)
    const = lambda si, b: (0, 0)
    tab = lambda si, b: (si, 0)

    def full(a):
        return pl.BlockSpec(a.shape, const)

    bf = jnp.bfloat16
    cast_specs = [_cast_block_spec(w.shape[0], w.shape[1], ns * batch, batch) for w in casts]
    widths = (MLA_HEADS * MLA_NOPE + LANES * (MLA_HEADS // 2), MLA_HEADS * MLA_NOPE + LANES,
              GQA_HEADS * GQA_HEAD_DIM, GQA_KV_HEADS * GQA_HEAD_DIM)
    vt_rows = N_V_HEADS * VT_ROWS
    return pl.pallas_call(
        _proj_kernel,
        grid=(ns, batch),
        in_specs=[pl.BlockSpec((tm, d_model), row), full(g1), full(win), full(winvt), full(gq),
                  full(wq), full(gkv), full(wkvk), full(wkvvt), full(ggq), full(ggk), full(sel)]
                 + [pl.BlockSpec((tm, LANES), tab) for _ in tables] + cast_specs,
        out_specs=[pl.BlockSpec((tm, w), row) for w in widths]
                  + [pl.BlockSpec((None, vt_rows, tm), lambda si, b: (b, 0, si)),
                     pl.BlockSpec((None, None, KN_ROWS, LANES), lambda si, b: (si, b, 0, 0))]
                  + cast_specs,
        out_shape=[jax.ShapeDtypeStruct((tokens, w), bf) for w in widths]
                  + [jax.ShapeDtypeStruct((batch, vt_rows, seq), bf),
                     jax.ShapeDtypeStruct((ns, batch, KN_ROWS, LANES), jnp.float32)]
                  + [jax.ShapeDtypeStruct(w.shape, bf) for w in casts],
        compiler_params=pltpu.CompilerParams(
            dimension_semantics=("arbitrary", "arbitrary"), vmem_limit_bytes=VMEM_LIMIT),
        name="proj",
    )(x2, g1, win, winvt, gq, wq, gkv, wkvk, wkvvt, ggq, ggk, sel, *tables, *casts)


def _attn_kernel(qm_ref, qg_ref, km_ref, kg_ref, vt_ref, kmax_ref, x_ref, gnm_ref, gng_ref, wo_ref,
                 h_ref, st_ref, p_ref, ot_ref, o_ref):
    tq = qm_ref.shape[0]
    seq = km_ref.shape[0]
    kc = min(KEY_CHUNK, seq)
    n_chunks = seq // kc
    bf = jnp.bfloat16
    nope_w = MLA_HEADS * MLA_NOPE
    lane = lax.broadcasted_iota(jnp.int32, (1, LANES), 1)
    first_of_pair = (lane % (LANES // 2)) < (MLA_ROPE // 2)

    def mla_operands(h, keys):
        own = first_of_pair if h % 2 == 0 else jnp.logical_not(first_of_pair)
        rope_lo = nope_w + LANES * (h // 2)
        q_rope = jnp.where(own, qm_ref[:, rope_lo:rope_lo + LANES], jnp.zeros((), bf))
        q = jnp.concatenate([qm_ref[:, LANES * h:LANES * (h + 1)], q_rope], axis=1)
        k = jnp.concatenate([km_ref[keys, LANES * h:LANES * (h + 1)],
                             km_ref[keys, nope_w:nope_w + LANES]], axis=1)
        return k, q

    def gqa_operands(h, keys):
        j = h // GQA_GROUP
        return kg_ref[keys, LANES * j:LANES * (j + 1)], qg_ref[:, LANES * h:LANES * (h + 1)]

    units = ([(mla_operands, h, VT_ROWS * h) for h in range(MLA_HEADS)]
             + [(gqa_operands, h, VT_ROWS * (MLA_HEADS + h // GQA_GROUP)) for h in range(GQA_HEADS)])
    n_units = len(units)
    min_denom = []

    def weighted_values(u, c):
        vt_lo = units[u][2]
        keys = slice(kc * c, kc * (c + 1))
        pv = _dot(vt_ref[vt_lo:vt_lo + VT_ROWS, keys], p_ref[u % 2, keys, :])
        if c != 0:
            pv += ot_ref[u % 2]
        if c != n_chunks - 1:
            ot_ref[u % 2] = pv
        else:
            denom = pv[MLA_V:MLA_V + 1]
            min_denom.append(denom)
            o_ref[:, LANES * u:LANES * (u + 1)] = (pv[:MLA_V] / denom).T

    mla_w, gqa_w = MLA_HEADS * MLA_V, GQA_HEADS * GQA_HEAD_DIM

    def out_proj(lo, width, gain_ref):
        o = _rms(o_ref[:, lo:lo + width], gain_ref[...]).astype(bf)
        return _dot(o, wo_ref[lo:lo + width, :])

    def bound_shifted_pipeline():
        shift = [None] * n_units

        def probs(u, c):
            operands, h, vt_lo = units[u]
            k, q = operands(h, slice(kc * c, kc * (c + 1)))
            if c == 0:
                q32 = q.astype(jnp.float32)
                q_norm2 = _dot_t(jnp.ones((BF16_ROWS, q.shape[1]), bf), (q32 * q32).astype(bf))[:1]
                key_head = vt_lo // VT_ROWS
                k_norm = jnp.concatenate([kmax_ref[key_head:key_head + 1, :]] * (tq // LANES), axis=1)
                shift[u] = jnp.broadcast_to(jnp.sqrt(q_norm2) * k_norm, (8, tq))
            s = _dot_t(k, q).reshape(kc // 8, 8, tq)
            p_ref[u % 2, kc * c:kc * (c + 1), :] = jnp.exp2(s - shift[u][None]).reshape(kc, tq).astype(bf)

        for t in range(n_units + 1):
            if t == MLA_HEADS + 1:
                h_ref[...] = x_ref[...] + out_proj(0, mla_w, gnm_ref)
            for c in range(n_chunks):
                if t < n_units:
                    probs(t, c)
                if t >= 1:
                    weighted_values(t - 1, c)
        smallest = jnp.min(functools.reduce(jnp.minimum, min_denom))
        largest = jnp.max(functools.reduce(jnp.maximum, min_denom))
        del min_denom[:]
        return smallest, largest

    def row_max_pipeline():
        col_max = [None] * n_units

        def scores(u, c):
            operands, h, _ = units[u]
            s = _dot_t(*operands(h, slice(kc * c, kc * (c + 1))))
            st_ref[u % 2, kc * c:kc * (c + 1), :] = s
            cm = jnp.max(s.reshape(kc // 8, 8, tq), axis=0)
            col_max[u] = cm if c == 0 else jnp.maximum(col_max[u], cm)
            if c == n_chunks - 1:
                col_max[u] = jnp.broadcast_to(jnp.max(col_max[u], axis=0, keepdims=True), (8, tq))

        def probs(u, c):
            s = st_ref[u % 2, kc * c:kc * (c + 1), :].reshape(kc // 8, 8, tq)
            p = jnp.exp2(s - col_max[u][None]).reshape(kc, tq)
            p_ref[u % 2, kc * c:kc * (c + 1), :] = p.astype(bf)

        for t in range(n_units + PV_LAG):
            for c in range(n_chunks):
                if t < n_units:
                    scores(t, c)
                if 0 <= t - 1 < n_units:
                    probs(t - 1, c)
                if 0 <= t - PV_LAG < n_units:
                    weighted_values(t - PV_LAG, c)
        del min_denom[:]

    smallest, largest = bound_shifted_pipeline()
    in_range = jnp.logical_and(smallest >= MIN_DENOM, largest <= MAX_DENOM)

    @pl.when(in_range)
    def _():
        h_ref[...] += out_proj(mla_w, gqa_w, gng_ref)

    @pl.when(jnp.logical_not(in_range))
    def _():
        row_max_pipeline()
        h_ref[...] = x_ref[...] + out_proj(0, mla_w, gnm_ref) + out_proj(mla_w, gqa_w, gng_ref)


def _attention(qm, qg, km, kg, vt, kmax, x2, gnm, gng, wo, batch, seq, tq):
    tokens, d_model = x2.shape
    nq = seq // tq
    row = lambda b, qi: (b * nq + qi, 0)
    per_batch = lambda b, qi: (b, 0)
    const = lambda b, qi: (0, 0)

    def full(a):
        return pl.BlockSpec(a.shape, const)

    return pl.pallas_call(
        _attn_kernel,
        grid=(batch, nq),
        in_specs=[pl.BlockSpec((tq, qm.shape[1]), row), pl.BlockSpec((tq, qg.shape[1]), row),
                  pl.BlockSpec((seq, km.shape[1]), per_batch),
                  pl.BlockSpec((seq, kg.shape[1]), per_batch),
                  pl.BlockSpec((None, vt.shape[1], seq), lambda b, qi: (b, 0, 0)),
                  pl.BlockSpec((None, KN_ROWS, LANES), lambda b, qi: (b, 0, 0)),
                  pl.BlockSpec((tq, d_model), row), full(gnm), full(gng), full(wo)],
        out_specs=pl.BlockSpec((tq, d_model), row),
        out_shape=jax.ShapeDtypeStruct((tokens, d_model), jnp.float32),
        scratch_shapes=[pltpu.VMEM((2, seq, tq), jnp.float32),
                        pltpu.VMEM((2, seq, tq), jnp.bfloat16),
                        pltpu.VMEM((2, VT_ROWS, tq), jnp.float32),
                        pltpu.VMEM((tq, d_model), jnp.float32)],
        compiler_params=pltpu.CompilerParams(
            dimension_semantics=("arbitrary", "arbitrary"), vmem_limit_bytes=VMEM_LIMIT),
        name="attn",
    )(qm, qg, km, kg, vt, kmax, x2, gnm, gng, wo)


def _ffn_kernel(h_ref, hp_ref, hn_ref, g2_ref, wup_ref, cw_ref, cb_ref, wdn_ref, gf_ref,
                o_ref, u_ref, hu_ref, act_ref, *, tiles_per_seq, final_norm):
    tm = h_ref.shape[0]
    bf = jnp.bfloat16
    i = pl.program_id(0)
    si = i % tiles_per_seq
    keep_prev = (si != 0).astype(jnp.float32)
    keep_next = (si != tiles_per_seq - 1).astype(jnp.float32)
    g2 = g2_ref[...]
    u_ref[...] = jnp.concatenate([_rms(hp_ref[...], g2) * keep_prev, _rms(h_ref[...], g2),
                                  _rms(hn_ref[...], g2) * keep_next], axis=0).astype(bf)
    lane_tiles = FF_CHUNK // LANES

    def col(part, c, j=0):
        return part * D_FF + FF_CHUNK * c + LANES * j

    def up(c, part):
        t = _dot(u_ref[...], wup_ref[:, col(part, c):col(part, c) + FF_CHUNK])
        for j in range(lane_tiles):
            hu_ref[c % 2, part, j] = t[:, LANES * j:LANES * (j + 1)]

    def conv(c, part, j):
        lo = col(part, c, j)
        w = cw_ref[:, lo:lo + LANES]
        src = hu_ref.at[c % 2, part, j]
        return (src[pl.ds(HALO - 1, tm), :] * w[0:1] + src[pl.ds(HALO, tm), :] * w[1:2]
                + src[pl.ds(HALO + 1, tm), :] * w[2:3] + cb_ref[:, lo:lo + LANES])

    def act(c, j):
        gate = conv(c, 0, j)
        lo = col(0, c, j)
        act_ref[:, lo:lo + LANES] = (gate * jax.nn.sigmoid(gate) * conv(c, 1, j)).astype(bf)

    assert lane_tiles == 2
    up(0, 0)
    up(0, 1)
    for c in range(N_FF_CHUNKS):
        for half in range(2):
            if c + 1 < N_FF_CHUNKS:
                up(c + 1, half)
            act(c, half)
    out = h_ref[...] + _dot(act_ref[...], wdn_ref[...])
    if final_norm:
        out = _rms(out, gf_ref[...])
    o_ref[...] = out


def _conv_ffn(h2, g2, wup, cw, cb, wdn, gf, seq, tm, final_norm):
    tokens, d_model = h2.shape
    assert (tm + 2 * HALO) % BF16_ROWS == 0
    tiles_per_seq = seq // tm
    n_halo_blocks = tokens // HALO
    per_tile = tm // HALO
    const = lambda i: (0, 0)
    kern = functools.partial(_ffn_kernel, tiles_per_seq=tiles_per_seq, final_norm=final_norm)
    return pl.pallas_call(
        kern,
        grid=(tokens // tm,),
        in_specs=[pl.BlockSpec((tm, d_model), lambda i: (i, 0)),
                  pl.BlockSpec((HALO, d_model), lambda i: (jnp.maximum(i * per_tile - 1, 0), 0)),
                  pl.BlockSpec((HALO, d_model),
                               lambda i: (jnp.minimum((i + 1) * per_tile, n_halo_blocks - 1), 0)),
                  pl.BlockSpec(g2.shape, const),
                  pl.BlockSpec(wup.shape, const, pipeline_mode=pl.Buffered(1)),
                  pl.BlockSpec(cw.shape, const), pl.BlockSpec(cb.shape, const),
                  pl.BlockSpec(wdn.shape, const, pipeline_mode=pl.Buffered(1)),
                  pl.BlockSpec(gf.shape, const)],
        out_specs=pl.BlockSpec((tm, d_model), lambda i: (i, 0)),
        out_shape=jax.ShapeDtypeStruct((tokens, d_model), jnp.float32),
        scratch_shapes=[pltpu.VMEM((tm + 2 * HALO, d_model), jnp.bfloat16),
                        pltpu.VMEM((2, 2, FF_CHUNK // LANES, tm + 2 * HALO, LANES), jnp.float32),
                        pltpu.VMEM((tm, D_FF), jnp.bfloat16)],
        compiler_params=pltpu.CompilerParams(
            dimension_semantics=("arbitrary",), vmem_limit_bytes=VMEM_LIMIT),
        name="convffn",
    )(h2, h2, h2, g2, wup, cw, cb, wdn, gf)


def _rope_np(pos, dim):
    inv = np.float32(ROPE_THETA) ** (-np.arange(0, dim, 2, dtype=np.float32) / np.float32(dim))
    ang = pos.astype(np.float32)[:, None] * inv[None, :]
    return np.cos(ang), np.sin(ang)


def _lane_tables(cos_a, sin_a, cos_b, sin_b):
    c = np.concatenate([cos_a, cos_b, cos_a, cos_b], axis=-1)
    s = np.concatenate([-sin_a, -sin_b, sin_a, sin_b], axis=-1)
    return jnp.asarray(c), jnp.asarray(s)


def _interleave_pairs(w, width):
    lead = w.shape[:-1]
    w = w.reshape(lead + (-1, 2, 2, width // 2))
    return jnp.swapaxes(w, -3, -2).reshape(lead + (-1,))


def _pick_tile(seq, want):
    t = min(seq, want)
    assert seq % t == 0 and t % HALO == 0
    return t


def kernel(x, norm1_g, w_in, mla_q_norm_g, mla_w_q_up, mla_kv_norm_g, mla_w_kv_up, gqa_q_norm_g,
           gqa_k_norm_g, group_norm_mla_g, group_norm_gqa_g, w_out, norm2_g, ffn_w_up, ffn_conv_w,
           ffn_conv_b, ffn_w_down, final_norm_g):
    batch, seq, d_model = x.shape
    depth = w_in.shape[0]
    bf = jnp.bfloat16
    tm_proj = _pick_tile(seq, PROJ_ROWS)
    tq = _pick_tile(seq, ATTN_QUERIES)
    tm_ffn = _pick_tile(seq, FFN_ROWS)

    t = np.arange(seq)
    cos1, sin1 = _rope_np(t, MLA_ROPE)
    cos_r, sin_r = _rope_np(t // GRID_W, GQA_HEAD_DIM // 2)
    cos_c, sin_c = _rope_np(t % GRID_W, GQA_HEAD_DIM // 2)
    tables = _lane_tables(cos1, sin1, cos1, sin1) + _lane_tables(cos_r, sin_r, cos_c, sin_c)

    sel = np.zeros((MLA_HEADS * MLA_NOPE + LANES, LANES), np.float32)
    rope_lane = np.arange(LANES)
    for h in range(MLA_HEADS):
        sel[MLA_NOPE * h:MLA_NOPE * (h + 1), h] = 1.0
        own = ((rope_lane % (LANES // 2)) < (MLA_ROPE // 2)) == (h % 2 == 0)
        sel[MLA_HEADS * MLA_NOPE + rope_lane[own], h] = 1.0
    sel = jnp.asarray(sel, bf)

    row = lambda v: v.reshape(1, -1)
    h2 = x.reshape(batch * seq, d_model)
    for l in range(depth):
        kpe_w = w_in[l][:, OFF_KPE:OFF_GQ]
        win = jnp.concatenate(
            [w_in[l][:, :OFF_KPE],
             _interleave_pairs(jnp.concatenate([kpe_w, kpe_w], axis=1), MLA_ROPE),
             _interleave_pairs(w_in[l][:, OFF_GQ:OFF_GV], GQA_HEAD_DIM // 2)],
            axis=1).astype(bf)
        winvt = w_in[l][:, OFF_GV:IN_WIDTH].T.astype(bf)
        wq3 = mla_w_q_up[l].reshape(MLA_Q_RANK, MLA_HEADS, MLA_QK)
        wq = jnp.concatenate(
            [wq3[:, :, :MLA_NOPE].reshape(MLA_Q_RANK, -1),
             _interleave_pairs(wq3[:, :, MLA_NOPE:].reshape(MLA_Q_RANK, -1), MLA_ROPE)],
            axis=1).astype(bf)
        wkv3 = mla_w_kv_up[l].reshape(MLA_KV_RANK, MLA_HEADS, MLA_NOPE + MLA_V)
        wkvk = wkv3[:, :, :MLA_NOPE].reshape(MLA_KV_RANK, -1).astype(bf)
        wkvvt = wkv3[:, :, MLA_NOPE:].reshape(MLA_KV_RANK, -1).T.astype(bf)
        qm, km, qg, kg, vt, kn2, wo, wup, wdn = _projections(
            h2, row(norm1_g[l]), win, winvt, row(mla_q_norm_g[l]), wq, row(mla_kv_norm_g[l]),
            wkvk, wkvvt, row(_interleave_pairs(gqa_q_norm_g[l], GQA_HEAD_DIM // 2)),
            row(_interleave_pairs(gqa_k_norm_g[l], GQA_HEAD_DIM // 2)), sel, tables,
            (w_out[l], ffn_w_up[l], ffn_w_down[l]), batch, seq, tm_proj)
        k_mla = jnp.sqrt(jnp.max(kn2[:, :, 0, :MLA_HEADS], axis=0))
        k_gqa = jnp.full((batch, GQA_KV_HEADS),
                         GQA_HEAD_DIM ** 0.5) * jnp.max(jnp.abs(gqa_k_norm_g[l]))
        kmax = jnp.concatenate(
            [k_mla, k_gqa, jnp.zeros((batch, KN_ROWS - N_V_HEADS), jnp.float32)], axis=1)
        kmax = jnp.broadcast_to(kmax[:, :, None], (batch, KN_ROWS, LANES))
        h2 = _attention(qm, qg, km, kg, vt, kmax, h2, row(group_norm_mla_g[l]),
                        row(group_norm_gqa_g[l]), wo, batch, seq, tq)
        h2 = _conv_ffn(h2, row(norm2_g[l]), wup, ffn_conv_w[l], row(ffn_conv_b[l]), wdn,
                       row(final_norm_g), seq, tm_ffn, final_norm=(l == depth - 1))
    return h2.reshape(batch, seq, d_model)
```

```python
import functools

import numpy as np

import jax
import jax.numpy as jnp
from jax import lax
from jax.experimental import pallas as pl
from jax.experimental.pallas import tpu as pltpu

EPS = 1e-6
LOG2_E = 1.4426950408889634
ROPE_THETA = 10000.0
GRID_W = 64

D_MODEL = 1024
MLA_HEADS = 4
MLA_Q_RANK = 384
MLA_KV_RANK = 256
MLA_NOPE = 128
MLA_ROPE = 64
MLA_V = 128
MLA_QK = MLA_NOPE + MLA_ROPE

GQA_HEADS = 4
GQA_KV_HEADS = 2
GQA_HEAD_DIM = 128
GQA_GROUP = GQA_HEADS // GQA_KV_HEADS

D_FF = 2816
LANES = 128
BF16_ROWS = 16

OFF_KPE = MLA_Q_RANK + MLA_KV_RANK
OFF_GQ = OFF_KPE + MLA_ROPE
OFF_GK = OFF_GQ + GQA_HEADS * GQA_HEAD_DIM
OFF_GV = OFF_GK + GQA_KV_HEADS * GQA_HEAD_DIM
IN_WIDTH = OFF_GV + GQA_KV_HEADS * GQA_HEAD_DIM

Z_CQ = 0
Z_CKV = Z_CQ + MLA_Q_RANK
Z_KPE = Z_CKV + MLA_KV_RANK
Z_GQ = Z_KPE + LANES
Z_GK = Z_GQ + GQA_HEADS * GQA_HEAD_DIM
Z_WIDTH = Z_GK + GQA_KV_HEADS * GQA_HEAD_DIM

N_V_HEADS = MLA_HEADS + GQA_KV_HEADS
VT_ROWS = MLA_V + BF16_ROWS
KN_ROWS = 8
MIN_DENOM = 2.0 ** -60
MAX_DENOM = 2.0 ** 40

FF_CHUNK = 256
N_FF_CHUNKS = D_FF // FF_CHUNK
HALO = 8
KEY_CHUNK = 2048
PV_LAG = 2
PROJ_SPLIT = 4

PROJ_ROWS = 1024
ATTN_QUERIES = 512
FFN_ROWS = 512

V7X_VMEM_BYTES = 64 * 1024 * 1024
VMEM_LIMIT = V7X_VMEM_BYTES * 7 // 8


def _rms(x, g):
    y = x * lax.rsqrt(jnp.mean(x * x, axis=-1, keepdims=True) + EPS)
    return y * g


def _rope(x, c, s):
    return x * c + pltpu.roll(x, LANES // 2, 1) * s


def _dot(a, b):
    return jnp.dot(a, b, preferred_element_type=jnp.float32)


def _dot_t(a, b):
    return lax.dot_general(a, b, (((1,), (1,)), ((), ())), preferred_element_type=jnp.float32)


def _proj_kernel(x_ref, g1_ref, win_ref, winvt_ref, gq_ref, wq_ref, gkv_ref, wkvk_ref, wkvvt_ref,
                 ggq_ref, ggk_ref, sel_ref, c1_ref, s1_ref, cg_ref, sg_ref, *refs):
    n_casts = (len(refs) - 6) // 2
    qm_ref, km_ref, qg_ref, kg_ref, vt_ref, kn2_ref = refs[n_casts:n_casts + 6]
    bf = jnp.bfloat16
    for src, dst in zip(refs[:n_casts], refs[n_casts + 6:]):
        dst[...] = src[...].astype(bf)
    tm = x_ref.shape[0]
    rows_per = tm // PROJ_SPLIT
    scale_m = MLA_QK ** -0.5 * LOG2_E
    scale_g = GQA_HEAD_DIM ** -0.5 * LOG2_E
    nope_w = MLA_HEADS * MLA_NOPE

    for r in range(PROJ_SPLIT):
        rows = pl.ds(r * rows_per, rows_per)
        cols = slice(r * rows_per, (r + 1) * rows_per)
        u = _rms(x_ref[rows, :], g1_ref[...]).astype(bf)
        z = _dot(u, win_ref[...])
        c_q = _rms(z[:, Z_CQ:Z_CKV], gq_ref[...]).astype(bf)
        q = _dot(c_q, wq_ref[...])
        c_kv = _rms(z[:, Z_CKV:Z_KPE], gkv_ref[...]).astype(bf)
        k_nope = _dot(c_kv, wkvk_ref[...])
        vt_mla = _dot_t(wkvvt_ref[...], c_kv)
        vt_gqa = _dot_t(winvt_ref[...], u)

        c1, s1 = c1_ref[rows, :], s1_ref[rows, :]
        qm_ref[rows, :nope_w] = (q[:, :nope_w] * scale_m).astype(bf)
        for i in range(MLA_HEADS // 2):
            lo = nope_w + LANES * i
            qm_ref[rows, lo:lo + LANES] = (_rope(q[:, lo:lo + LANES], c1, s1) * scale_m).astype(bf)
        k_rope = _rope(z[:, Z_KPE:Z_GQ], c1, s1)
        km_ref[rows, :nope_w] = k_nope.astype(bf)
        km_ref[rows, nope_w:] = k_rope.astype(bf)
        if r == 0:
            k_sq = jnp.concatenate([k_nope * k_nope, k_rope * k_rope], axis=1).astype(bf)
            key_norm2 = jnp.max(_dot(k_sq, sel_ref[...]), axis=0, keepdims=True)
        for h in range(MLA_HEADS):
            vt_ref[VT_ROWS * h:VT_ROWS * h + MLA_V, cols] = vt_mla[MLA_V * h:MLA_V * (h + 1)].astype(bf)

        cg, sg = cg_ref[rows, :], sg_ref[rows, :]
        for h in range(GQA_HEADS):
            t = _rms(z[:, Z_GQ + LANES * h:Z_GQ + LANES * (h + 1)], ggq_ref[...])
            qg_ref[rows, LANES * h:LANES * (h + 1)] = (_rope(t, cg, sg) * scale_g).astype(bf)
        for j in range(GQA_KV_HEADS):
            t = _rms(z[:, Z_GK + LANES * j:Z_GK + LANES * (j + 1)], ggk_ref[...])
            kg_ref[rows, LANES * j:LANES * (j + 1)] = _rope(t, cg, sg).astype(bf)
            lo = VT_ROWS * (MLA_HEADS + j)
            vt_ref[lo:lo + MLA_V, cols] = vt_gqa[LANES * j:LANES * (j + 1)].astype(bf)

    ones = jnp.ones((BF16_ROWS, tm), bf)
    for h in range(N_V_HEADS):
        vt_ref[VT_ROWS * h + MLA_V:VT_ROWS * (h + 1), :] = ones
    kn2_ref[...] = jnp.broadcast_to(key_norm2, (KN_ROWS, LANES))


def _cast_block_spec(rows, cols, n_steps, batch):
    n_blocks = max(d for d in range(1, n_steps + 1) if (rows // BF16_ROWS) % d == 0)
    return pl.BlockSpec((rows // n_blocks, cols),
                        lambda si, b: (jnp.minimum(si * batch + b, n_blocks - 1), 0))


def _projections(x2, g1, win, winvt, gq, wq, gkv, wkvk, wkvvt, ggq, ggk, sel, tables, casts,
                 batch, seq, tm):
    tokens, d_model = x2.shape
    assert (tm // PROJ_SPLIT) % LANES == 0
    ns = seq // tm
    row = lambda si, b: (b * ns + si, 0)
    const = lambda si, b: (0, 0)
    tab = lambda si, b: (si, 0)

    def full(a):
        return pl.BlockSpec(a.shape, const)

    bf = jnp.bfloat16
    cast_specs = [_cast_block_spec(w.shape[0], w.shape[1], ns * batch, batch) for w in casts]
    widths = (MLA_HEADS * MLA_NOPE + LANES * (MLA_HEADS // 2), MLA_HEADS * MLA_NOPE + LANES,
              GQA_HEADS * GQA_HEAD_DIM, GQA_KV_HEADS * GQA_HEAD_DIM)
    vt_rows = N_V_HEADS * VT_ROWS
    return pl.pallas_call(
        _proj_kernel,
        grid=(ns, batch),
        in_specs=[pl.BlockSpec((tm, d_model), row), full(g1), full(win), full(winvt), full(gq),
                  full(wq), full(gkv), full(wkvk), full(wkvvt), full(ggq), full(ggk), full(sel)]
                 + [pl.BlockSpec((tm, LANES), tab) for _ in tables] + cast_specs,
        out_specs=[pl.BlockSpec((tm, w), row) for w in widths]
                  + [pl.BlockSpec((None, vt_rows, tm), lambda si, b: (b, 0, si)),
                     pl.BlockSpec((None, None, KN_ROWS, LANES), lambda si, b: (si, b, 0, 0))]
                  + cast_specs,
        out_shape=[jax.ShapeDtypeStruct((tokens, w), bf) for w in widths]
                  + [jax.ShapeDtypeStruct((batch, vt_rows, seq), bf),
                     jax.ShapeDtypeStruct((ns, batch, KN_ROWS, LANES), jnp.float32)]
                  + [jax.ShapeDtypeStruct(w.shape, bf) for w in casts],
        compiler_params=pltpu.CompilerParams(
            dimension_semantics=("arbitrary", "arbitrary"), vmem_limit_bytes=VMEM_LIMIT),
        name="proj",
    )(x2, g1, win, winvt, gq, wq, gkv, wkvk, wkvvt, ggq, ggk, sel, *tables, *casts)


def _attn_kernel(qm_ref, qg_ref, km_ref, kg_ref, vt_ref, kmax_ref, x_ref, gnm_ref, gng_ref, wo_ref,
                 h_ref, st_ref, p_ref, ot_ref, o_ref):
    tq = qm_ref.shape[0]
    seq = km_ref.shape[0]
    kc = min(KEY_CHUNK, seq)
    n_chunks = seq // kc
    bf = jnp.bfloat16
    nope_w = MLA_HEADS * MLA_NOPE
    lane = lax.broadcasted_iota(jnp.int32, (1, LANES), 1)
    first_of_pair = (lane % (LANES // 2)) < (MLA_ROPE // 2)

    def mla_operands(h, keys):
        own = first_of_pair if h % 2 == 0 else jnp.logical_not(first_of_pair)
        rope_lo = nope_w + LANES * (h // 2)
        q_rope = jnp.where(own, qm_ref[:, rope_lo:rope_lo + LANES], jnp.zeros((), bf))
        q = jnp.concatenate([qm_ref[:, LANES * h:LANES * (h + 1)], q_rope], axis=1)
        k = jnp.concatenate([km_ref[keys, LANES * h:LANES * (h + 1)],
                             km_ref[keys, nope_w:nope_w + LANES]], axis=1)
        return k, q

    def gqa_operands(h, keys):
        j = h // GQA_GROUP
        return kg_ref[keys, LANES * j:LANES * (j + 1)], qg_ref[:, LANES * h:LANES * (h + 1)]

    units = ([(mla_operands, h, VT_ROWS * h) for h in range(MLA_HEADS)]
             + [(gqa_operands, h, VT_ROWS * (MLA_HEADS + h // GQA_GROUP)) for h in range(GQA_HEADS)])
    n_units = len(units)
    min_denom = []

    def weighted_values(u, c):
        vt_lo = units[u][2]
        keys = slice(kc * c, kc * (c + 1))
        pv = _dot(vt_ref[vt_lo:vt_lo + VT_ROWS, keys], p_ref[u % 2, keys, :])
        if c != 0:
            pv += ot_ref[u % 2]
        if c != n_chunks - 1:
            ot_ref[u % 2] = pv
        else:
            denom = pv[MLA_V:MLA_V + 1]
            min_denom.append(denom)
            o_ref[:, LANES * u:LANES * (u + 1)] = (pv[:MLA_V] / denom).T

    def bound_shifted_pipeline():
        shift = [None] * n_units

        def probs(u, c):
            operands, h, vt_lo = units[u]
            k, q = operands(h, slice(kc * c, kc * (c + 1)))
            if c == 0:
                q32 = q.astype(jnp.float32)
                q_norm2 = _dot_t(jnp.ones((BF16_ROWS, q.shape[1]), bf), (q32 * q32).astype(bf))[:1]
                key_head = vt_lo // VT_ROWS
                k_norm = jnp.concatenate([kmax_ref[key_head:key_head + 1, :]] * (tq // LANES), axis=1)
                shift[u] = jnp.broadcast_to(jnp.sqrt(q_norm2) * k_norm, (8, tq))
            s = _dot_t(k, q).reshape(kc // 8, 8, tq)
            p_ref[u % 2, kc * c:kc * (c + 1), :] = jnp.exp2(s - shift[u][None]).reshape(kc, tq).astype(bf)

        for t in range(n_units + 1):
            for c in range(n_chunks):
                if t < n_units:
                    probs(t, c)
                if t >= 1:
                    weighted_values(t - 1, c)
        smallest = jnp.min(functools.reduce(jnp.minimum, min_denom))
        largest = jnp.max(functools.reduce(jnp.maximum, min_denom))
        del min_denom[:]
        return smallest, largest

    def row_max_pipeline():
        col_max = [None] * n_units

        def scores(u, c):
            operands, h, _ = units[u]
            s = _dot_t(*operands(h, slice(kc * c, kc * (c + 1))))
            st_ref[u % 2, kc * c:kc * (c + 1), :] = s
            cm = jnp.max(s.reshape(kc // 8, 8, tq), axis=0)
            col_max[u] = cm if c == 0 else jnp.maximum(col_max[u], cm)
            if c == n_chunks - 1:
                col_max[u] = jnp.broadcast_to(jnp.max(col_max[u], axis=0, keepdims=True), (8, tq))

        def probs(u, c):
            s = st_ref[u % 2, kc * c:kc * (c + 1), :].reshape(kc // 8, 8, tq)
            p = jnp.exp2(s - col_max[u][None]).reshape(kc, tq)
            p_ref[u % 2, kc * c:kc * (c + 1), :] = p.astype(bf)

        for t in range(n_units + PV_LAG):
            for c in range(n_chunks):
                if t < n_units:
                    scores(t, c)
                if 0 <= t - 1 < n_units:
                    probs(t - 1, c)
                if 0 <= t - PV_LAG < n_units:
                    weighted_values(t - PV_LAG, c)
        del min_denom[:]

    smallest, largest = bound_shifted_pipeline()
    in_range = jnp.logical_and(smallest >= MIN_DENOM, largest <= MAX_DENOM)

    def epilogue():
        mla_w = MLA_HEADS * MLA_V
        o = jnp.concatenate([_rms(o_ref[:, :mla_w], gnm_ref[...]),
                             _rms(o_ref[:, mla_w:], gng_ref[...])], axis=-1)
        h_ref[...] = x_ref[...] + _dot(o.astype(bf), wo_ref[...])

    epilogue()

    @pl.when(jnp.logical_not(in_range))
    def _():
        row_max_pipeline()
        epilogue()


def _attention(qm, qg, km, kg, vt, kmax, x2, gnm, gng, wo, batch, seq, tq):
    tokens, d_model = x2.shape
    nq = seq // tq
    row = lambda b, qi: (b * nq + qi, 0)
    per_batch = lambda b, qi: (b, 0)
    const = lambda b, qi: (0, 0)

    def full(a):
        return pl.BlockSpec(a.shape, const)

    return pl.pallas_call(
        _attn_kernel,
        grid=(batch, nq),
        in_specs=[pl.BlockSpec((tq, qm.shape[1]), row), pl.BlockSpec((tq, qg.shape[1]), row),
                  pl.BlockSpec((seq, km.shape[1]), per_batch),
                  pl.BlockSpec((seq, kg.shape[1]), per_batch),
                  pl.BlockSpec((None, vt.shape[1], seq), lambda b, qi: (b, 0, 0)),
                  pl.BlockSpec((None, KN_ROWS, LANES), lambda b, qi: (b, 0, 0)),
                  pl.BlockSpec((tq, d_model), row), full(gnm), full(gng), full(wo)],
        out_specs=pl.BlockSpec((tq, d_model), row),
        out_shape=jax.ShapeDtypeStruct((tokens, d_model), jnp.float32),
        scratch_shapes=[pltpu.VMEM((2, seq, tq), jnp.float32),
                        pltpu.VMEM((2, seq, tq), jnp.bfloat16),
                        pltpu.VMEM((2, VT_ROWS, tq), jnp.float32),
                        pltpu.VMEM((tq, d_model), jnp.float32)],
        compiler_params=pltpu.CompilerParams(
            dimension_semantics=("arbitrary", "arbitrary"), vmem_limit_bytes=VMEM_LIMIT),
        name="attn",
    )(qm, qg, km, kg, vt, kmax, x2, gnm, gng, wo)


def _ffn_kernel(h_ref, hp_ref, hn_ref, g2_ref, wup_ref, cw_ref, cb_ref, wdn_ref, gf_ref,
                o_ref, u_ref, hu_ref, act_ref, *, tiles_per_seq, final_norm):
    tm = h_ref.shape[0]
    bf = jnp.bfloat16
    i = pl.program_id(0)
    si = i % tiles_per_seq
    keep_prev = (si != 0).astype(jnp.float32)
    keep_next = (si != tiles_per_seq - 1).astype(jnp.float32)
    g2 = g2_ref[...]
    u_ref[...] = jnp.concatenate([_rms(hp_ref[...], g2) * keep_prev, _rms(h_ref[...], g2),
                                  _rms(hn_ref[...], g2) * keep_next], axis=0).astype(bf)
    lane_tiles = FF_CHUNK // LANES

    def col(part, c, j=0):
        return part * D_FF + FF_CHUNK * c + LANES * j

    def up(c, part):
        t = _dot(u_ref[...], wup_ref[:, col(part, c):col(part, c) + FF_CHUNK])
        for j in range(lane_tiles):
            hu_ref[c % 2, part, j] = t[:, LANES * j:LANES * (j + 1)]

    def conv(c, part, j):
        lo = col(part, c, j)
        w = cw_ref[:, lo:lo + LANES]
        src = hu_ref.at[c % 2, part, j]
        return (src[pl.ds(HALO - 1, tm), :] * w[0:1] + src[pl.ds(HALO, tm), :] * w[1:2]
                + src[pl.ds(HALO + 1, tm), :] * w[2:3] + cb_ref[:, lo:lo + LANES])

    def act(c, j):
        gate = conv(c, 0, j)
        lo = col(0, c, j)
        act_ref[:, lo:lo + LANES] = (gate * jax.nn.sigmoid(gate) * conv(c, 1, j)).astype(bf)

    assert lane_tiles == 2
    up(0, 0)
    up(0, 1)
    for c in range(N_FF_CHUNKS):
        for half in range(2):
            if c + 1 < N_FF_CHUNKS:
                up(c + 1, half)
            act(c, half)
    out = h_ref[...] + _dot(act_ref[...], wdn_ref[...])
    if final_norm:
        out = _rms(out, gf_ref[...])
    o_ref[...] = out


def _conv_ffn(h2, g2, wup, cw, cb, wdn, gf, seq, tm, final_norm):
    tokens, d_model = h2.shape
    assert (tm + 2 * HALO) % BF16_ROWS == 0
    tiles_per_seq = seq // tm
    n_halo_blocks = tokens // HALO
    per_tile = tm // HALO
    const = lambda i: (0, 0)
    kern = functools.partial(_ffn_kernel, tiles_per_seq=tiles_per_seq, final_norm=final_norm)
    return pl.pallas_call(
        kern,
        grid=(tokens // tm,),
        in_specs=[pl.BlockSpec((tm, d_model), lambda i: (i, 0)),
                  pl.BlockSpec((HALO, d_model), lambda i: (jnp.maximum(i * per_tile - 1, 0), 0)),
                  pl.BlockSpec((HALO, d_model),
                               lambda i: (jnp.minimum((i + 1) * per_tile, n_halo_blocks - 1), 0)),
                  pl.BlockSpec(g2.shape, const),
                  pl.BlockSpec(wup.shape, const, pipeline_mode=pl.Buffered(1)),
                  pl.BlockSpec(cw.shape, const), pl.BlockSpec(cb.shape, const),
                  pl.BlockSpec(wdn.shape, const, pipeline_mode=pl.Buffered(1)),
                  pl.BlockSpec(gf.shape, const)],
        out_specs=pl.BlockSpec((tm, d_model), lambda i: (i, 0)),
        out_shape=jax.ShapeDtypeStruct((tokens, d_model), jnp.float32),
        scratch_shapes=[pltpu.VMEM((tm + 2 * HALO, d_model), jnp.bfloat16),
                        pltpu.VMEM((2, 2, FF_CHUNK // LANES, tm + 2 * HALO, LANES), jnp.float32),
                        pltpu.VMEM((tm, D_FF), jnp.bfloat16)],
        compiler_params=pltpu.CompilerParams(
            dimension_semantics=("arbitrary",), vmem_limit_bytes=VMEM_LIMIT),
        name="convffn",
    )(h2, h2, h2, g2, wup, cw, cb, wdn, gf)


def _rope_np(pos, dim):
    inv = np.float32(ROPE_THETA) ** (-np.arange(0, dim, 2, dtype=np.float32) / np.float32(dim))
    ang = pos.astype(np.float32)[:, None] * inv[None, :]
    return np.cos(ang), np.sin(ang)


def _lane_tables(cos_a, sin_a, cos_b, sin_b):
    c = np.concatenate([cos_a, cos_b, cos_a, cos_b], axis=-1)
    s = np.concatenate([-sin_a, -sin_b, sin_a, sin_b], axis=-1)
    return jnp.asarray(c), jnp.asarray(s)


def _interleave_pairs(w, width):
    lead = w.shape[:-1]
    w = w.reshape(lead + (-1, 2, 2, width // 2))
    return jnp.swapaxes(w, -3, -2).reshape(lead + (-1,))


def _pick_tile(seq, want):
    t = min(seq, want)
    assert seq % t == 0 and t % HALO == 0
    return t


def kernel(x, norm1_g, w_in, mla_q_norm_g, mla_w_q_up, mla_kv_norm_g, mla_w_kv_up, gqa_q_norm_g,
           gqa_k_norm_g, group_norm_mla_g, group_norm_gqa_g, w_out, norm2_g, ffn_w_up, ffn_conv_w,
           ffn_conv_b, ffn_w_down, final_norm_g):
    batch, seq, d_model = x.shape
    depth = w_in.shape[0]
    bf = jnp.bfloat16
    tm_proj = _pick_tile(seq, PROJ_ROWS)
    tq = _pick_tile(seq, ATTN_QUERIES)
    tm_ffn = _pick_tile(seq, FFN_ROWS)

    t = np.arange(seq)
    cos1, sin1 = _rope_np(t, MLA_ROPE)
    cos_r, sin_r = _rope_np(t // GRID_W, GQA_HEAD_DIM // 2)
    cos_c, sin_c = _rope_np(t % GRID_W, GQA_HEAD_DIM // 2)
    tables = _lane_tables(cos1, sin1, cos1, sin1) + _lane_tables(cos_r, sin_r, cos_c, sin_c)

    sel = np.zeros((MLA_HEADS * MLA_NOPE + LANES, LANES), np.float32)
    rope_lane = np.arange(LANES)
    for h in range(MLA_HEADS):
        sel[MLA_NOPE * h:MLA_NOPE * (h + 1), h] = 1.0
        own = ((rope_lane % (LANES // 2)) < (MLA_ROPE // 2)) == (h % 2 == 0)
        sel[MLA_HEADS * MLA_NOPE + rope_lane[own], h] = 1.0
    sel = jnp.asarray(sel, bf)

    row = lambda v: v.reshape(1, -1)
    h2 = x.reshape(batch * seq, d_model)
    for l in range(depth):
        kpe_w = w_in[l][:, OFF_KPE:OFF_GQ]
        win = jnp.concatenate(
            [w_in[l][:, :OFF_KPE],
             _interleave_pairs(jnp.concatenate([kpe_w, kpe_w], axis=1), MLA_ROPE),
             _interleave_pairs(w_in[l][:, OFF_GQ:OFF_GV], GQA_HEAD_DIM // 2)],
            axis=1).astype(bf)
        winvt = w_in[l][:, OFF_GV:IN_WIDTH].T.astype(bf)
        wq3 = mla_w_q_up[l].reshape(MLA_Q_RANK, MLA_HEADS, MLA_QK)
        wq = jnp.concatenate(
            [wq3[:, :, :MLA_NOPE].reshape(MLA_Q_RANK, -1),
             _interleave_pairs(wq3[:, :, MLA_NOPE:].reshape(MLA_Q_RANK, -1), MLA_ROPE)],
            axis=1).astype(bf)
        wkv3 = mla_w_kv_up[l].reshape(MLA_KV_RANK, MLA_HEADS, MLA_NOPE + MLA_V)
        wkvk = wkv3[:, :, :MLA_NOPE].reshape(MLA_KV_RANK, -1).astype(bf)
        wkvvt = wkv3[:, :, MLA_NOPE:].reshape(MLA_KV_RANK, -1).T.astype(bf)
        qm, km, qg, kg, vt, kn2, wo, wup, wdn = _projections(
            h2, row(norm1_g[l]), win, winvt, row(mla_q_norm_g[l]), wq, row(mla_kv_norm_g[l]),
            wkvk, wkvvt, row(_interleave_pairs(gqa_q_norm_g[l], GQA_HEAD_DIM // 2)),
            row(_interleave_pairs(gqa_k_norm_g[l], GQA_HEAD_DIM // 2)), sel, tables,
            (w_out[l], ffn_w_up[l], ffn_w_down[l]), batch, seq, tm_proj)
        k_mla = jnp.sqrt(jnp.max(kn2[:, :, 0, :MLA_HEADS], axis=0))
        k_gqa = jnp.full((batch, GQA_KV_HEADS),
                         GQA_HEAD_DIM ** 0.5) * jnp.max(jnp.abs(gqa_k_norm_g[l]))
        kmax = jnp.concatenate(
            [k_mla, k_gqa, jnp.zeros((batch, KN_ROWS - N_V_HEADS), jnp.float32)], axis=1)
        kmax = jnp.broadcast_to(kmax[:, :, None], (batch, KN_ROWS, LANES))
        h2 = _attention(qm, qg, km, kg, vt, kmax, h2, row(group_norm_mla_g[l]),
                        row(group_norm_gqa_g[l]), wo, batch, seq, tq)
        h2 = _conv_ffn(h2, row(norm2_g[l]), wup, ffn_conv_w[l], row(ffn_conv_b[l]), wdn,
                       row(final_norm_g), seq, tm_ffn, final_norm=(l == depth - 1))
    return h2.reshape(batch, seq, d_model)
```

```python
import functools

import numpy as np

import jax
import jax.numpy as jnp
from jax import lax
from jax.experimental import pallas as pl
from jax.experimental.pallas import tpu as pltpu

EPS = 1e-6
LOG2_E = 1.4426950408889634
ROPE_THETA = 10000.0
GRID_W = 64

D_MODEL = 1024
MLA_HEADS = 4
MLA_Q_RANK = 384
MLA_KV_RANK = 256
MLA_NOPE = 128
MLA_ROPE = 64
MLA_V = 128
MLA_QK = MLA_NOPE + MLA_ROPE

GQA_HEADS = 4
GQA_KV_HEADS = 2
GQA_HEAD_DIM = 128
GQA_GROUP = GQA_HEADS // GQA_KV_HEADS

D_FF = 2816
LANES = 128
BF16_ROWS = 16

OFF_KPE = MLA_Q_RANK + MLA_KV_RANK
OFF_GQ = OFF_KPE + MLA_ROPE
OFF_GK = OFF_GQ + GQA_HEADS * GQA_HEAD_DIM
OFF_GV = OFF_GK + GQA_KV_HEADS * GQA_HEAD_DIM
IN_WIDTH = OFF_GV + GQA_KV_HEADS * GQA_HEAD_DIM

Z_CQ = 0
Z_CKV = Z_CQ + MLA_Q_RANK
Z_KPE = Z_CKV + MLA_KV_RANK
Z_GQ = Z_KPE + LANES
Z_GK = Z_GQ + GQA_HEADS * GQA_HEAD_DIM
Z_WIDTH = Z_GK + GQA_KV_HEADS * GQA_HEAD_DIM

N_V_HEADS = MLA_HEADS + GQA_KV_HEADS
VT_ROWS = MLA_V + BF16_ROWS
KN_ROWS = 8
MIN_DENOM = 2.0 ** -60
MAX_DENOM = 2.0 ** 40

FF_CHUNK = 256
N_FF_CHUNKS = D_FF // FF_CHUNK
HU_SLOTS = 4
HALO = 8
KEY_CHUNK = 2048
PV_LAG = 2
PROJ_SPLIT = 4

PROJ_ROWS = 1024
ATTN_QUERIES = 512
FFN_ROWS = 512

V7X_VMEM_BYTES = 64 * 1024 * 1024
VMEM_LIMIT = V7X_VMEM_BYTES * 7 // 8


def _rms(x, g):
    y = x * lax.rsqrt(jnp.mean(x * x, axis=-1, keepdims=True) + EPS)
    return y * g


def _rope(x, c, s):
    return x * c + pltpu.roll(x, LANES // 2, 1) * s


def _dot(a, b):
    return jnp.dot(a, b, preferred_element_type=jnp.float32)


def _dot_t(a, b):
    return lax.dot_general(a, b, (((1,), (1,)), ((), ())), preferred_element_type=jnp.float32)


def _proj_kernel(x_ref, g1_ref, win_ref, winvt_ref, gq_ref, wq_ref, gkv_ref, wkvk_ref, wkvvt_ref,
                 ggq_ref, ggk_ref, sel_ref, c1_ref, s1_ref, cg_ref, sg_ref, *refs):
    n_casts = (len(refs) - 6) // 2
    qm_ref, km_ref, qg_ref, kg_ref, vt_ref, kn2_ref = refs[n_casts:n_casts + 6]
    bf = jnp.bfloat16
    for src, dst in zip(refs[:n_casts], refs[n_casts + 6:]):
        dst[...] = src[...].astype(bf)
    tm = x_ref.shape[0]
    rows_per = tm // PROJ_SPLIT
    scale_m = MLA_QK ** -0.5 * LOG2_E
    scale_g = GQA_HEAD_DIM ** -0.5 * LOG2_E
    nope_w = MLA_HEADS * MLA_NOPE

    for r in range(PROJ_SPLIT):
        rows = pl.ds(r * rows_per, rows_per)
        cols = slice(r * rows_per, (r + 1) * rows_per)
        u = _rms(x_ref[rows, :], g1_ref[...]).astype(bf)
        z = _dot(u, win_ref[...])
        c_q = _rms(z[:, Z_CQ:Z_CKV], gq_ref[...]).astype(bf)
        q = _dot(c_q, wq_ref[...])
        c_kv = _rms(z[:, Z_CKV:Z_KPE], gkv_ref[...]).astype(bf)
        k_nope = _dot(c_kv, wkvk_ref[...])
        vt_mla = _dot_t(wkvvt_ref[...], c_kv)
        vt_gqa = _dot_t(winvt_ref[...], u)

        c1, s1 = c1_ref[rows, :], s1_ref[rows, :]
        qm_ref[rows, :nope_w] = (q[:, :nope_w] * scale_m).astype(bf)
        for i in range(MLA_HEADS // 2):
            lo = nope_w + LANES * i
            qm_ref[rows, lo:lo + LANES] = (_rope(q[:, lo:lo + LANES], c1, s1) * scale_m).astype(bf)
        k_rope = _rope(z[:, Z_KPE:Z_GQ], c1, s1)
        km_ref[rows, :nope_w] = k_nope.astype(bf)
        km_ref[rows, nope_w:] = k_rope.astype(bf)
        if r == 0:
            k_sq = jnp.concatenate([k_nope * k_nope, k_rope * k_rope], axis=1).astype(bf)
            key_norm2 = jnp.max(_dot(k_sq, sel_ref[...]), axis=0, keepdims=True)
        for h in range(MLA_HEADS):
            vt_ref[VT_ROWS * h:VT_ROWS * h + MLA_V, cols] = vt_mla[MLA_V * h:MLA_V * (h + 1)].astype(bf)

        cg, sg = cg_ref[rows, :], sg_ref[rows, :]
        for h in range(GQA_HEADS):
            t = _rms(z[:, Z_GQ + LANES * h:Z_GQ + LANES * (h + 1)], ggq_ref[...])
            qg_ref[rows, LANES * h:LANES * (h + 1)] = (_rope(t, cg, sg) * scale_g).astype(bf)
        for j in range(GQA_KV_HEADS):
            t = _rms(z[:, Z_GK + LANES * j:Z_GK + LANES * (j + 1)], ggk_ref[...])
            kg_ref[rows, LANES * j:LANES * (j + 1)] = _rope(t, cg, sg).astype(bf)
            lo = VT_ROWS * (MLA_HEADS + j)
            vt_ref[lo:lo + MLA_V, cols] = vt_gqa[LANES * j:LANES * (j + 1)].astype(bf)

    ones = jnp.ones((BF16_ROWS, tm), bf)
    for h in range(N_V_HEADS):
        vt_ref[VT_ROWS * h + MLA_V:VT_ROWS * (h + 1), :] = ones
    kn2_ref[...] = jnp.broadcast_to(key_norm2, (KN_ROWS, LANES))


def _cast_block_spec(rows, cols, n_steps, batch):
    n_blocks = max(d for d in range(1, n_steps + 1) if (rows // BF16_ROWS) % d == 0)
    return pl.BlockSpec((rows // n_blocks, cols),
                        lambda si, b: (jnp.minimum(si * batch + b, n_blocks - 1), 0))


def _projections(x2, g1, win, winvt, gq, wq, gkv, wkvk, wkvvt, ggq, ggk, sel, tables, casts,
                 batch, seq, tm):
    tokens, d_model = x2.shape
    assert (tm // PROJ_SPLIT) % LANES == 0
    ns = seq // tm
    row = lambda si, b: (b * ns + si, 0)
    const = lambda si, b: (0, 0)
    tab = lambda si, b: (si, 0)

    def full(a):
        return pl.BlockSpec(a.shape, const)

    bf = jnp.bfloat16
    cast_specs = [_cast_block_spec(w.shape[0], w.shape[1], ns * batch, batch) for w in casts]
    widths = (MLA_HEADS * MLA_NOPE + LANES * (MLA_HEADS // 2), MLA_HEADS * MLA_NOPE + LANES,
              GQA_HEADS * GQA_HEAD_DIM, GQA_KV_HEADS * GQA_HEAD_DIM)
    vt_rows = N_V_HEADS * VT_ROWS
    return pl.pallas_call(
        _proj_kernel,
        grid=(ns, batch),
        in_specs=[pl.BlockSpec((tm, d_model), row), full(g1), full(win), full(winvt), full(gq),
                  full(wq), full(gkv), full(wkvk), full(wkvvt), full(ggq), full(ggk), full(sel)]
                 + [pl.BlockSpec((tm, LANES), tab) for _ in tables] + cast_specs,
        out_specs=[pl.BlockSpec((tm, w), row) for w in widths]
                  + [pl.BlockSpec((None, vt_rows, tm), lambda si, b: (b, 0, si)),
                     pl.BlockSpec((None, None, KN_ROWS, LANES), lambda si, b: (si, b, 0, 0))]
                  + cast_specs,
        out_shape=[jax.ShapeDtypeStruct((tokens, w), bf) for w in widths]
                  + [jax.ShapeDtypeStruct((batch, vt_rows, seq), bf),
                     jax.ShapeDtypeStruct((ns, batch, KN_ROWS, LANES), jnp.float32)]
                  + [jax.ShapeDtypeStruct(w.shape, bf) for w in casts],
        compiler_params=pltpu.CompilerParams(
            dimension_semantics=("arbitrary", "arbitrary"), vmem_limit_bytes=VMEM_LIMIT),
        name="proj",
    )(x2, g1, win, winvt, gq, wq, gkv, wkvk, wkvvt, ggq, ggk, sel, *tables, *casts)


def _attn_kernel(qm_ref, qg_ref, km_ref, kg_ref, vt_ref, kmax_ref, x_ref, gnm_ref, gng_ref, wo_ref,
                 h_ref, st_ref, p_ref, ot_ref, o_ref):
    tq = qm_ref.shape[0]
    seq = km_ref.shape[0]
    kc = min(KEY_CHUNK, seq)
    n_chunks = seq // kc
    bf = jnp.bfloat16
    nope_w = MLA_HEADS * MLA_NOPE
    lane = lax.broadcasted_iota(jnp.int32, (1, LANES), 1)
    first_of_pair = (lane % (LANES // 2)) < (MLA_ROPE // 2)

    def mla_operands(h, keys):
        own = first_of_pair if h % 2 == 0 else jnp.logical_not(first_of_pair)
        rope_lo = nope_w + LANES * (h // 2)
        q_rope = jnp.where(own, qm_ref[:, rope_lo:rope_lo + LANES], jnp.zeros((), bf))
        q = jnp.concatenate([qm_ref[:, LANES * h:LANES * (h + 1)], q_rope], axis=1)
        k = jnp.concatenate([km_ref[keys, LANES * h:LANES * (h + 1)],
                             km_ref[keys, nope_w:nope_w + LANES]], axis=1)
        return k, q

    def gqa_operands(h, keys):
        j = h // GQA_GROUP
        return kg_ref[keys, LANES * j:LANES * (j + 1)], qg_ref[:, LANES * h:LANES * (h + 1)]

    units = ([(mla_operands, h, VT_ROWS * h) for h in range(MLA_HEADS)]
             + [(gqa_operands, h, VT_ROWS * (MLA_HEADS + h // GQA_GROUP)) for h in range(GQA_HEADS)])
    n_units = len(units)
    min_denom = []

    def weighted_values(u, c):
        vt_lo = units[u][2]
        keys = slice(kc * c, kc * (c + 1))
        pv = _dot(vt_ref[vt_lo:vt_lo + VT_ROWS, keys], p_ref[u % 2, keys, :])
        if c != 0:
            pv += ot_ref[u % 2]
        if c != n_chunks - 1:
            ot_ref[u % 2] = pv
        else:
            denom = pv[MLA_V:MLA_V + 1]
            min_denom.append(denom)
            o_ref[:, LANES * u:LANES * (u + 1)] = (pv[:MLA_V] / denom).T

    def bound_shifted_pipeline():
        shift = [None] * n_units

        def probs(u, c):
            operands, h, vt_lo = units[u]
            k, q = operands(h, slice(kc * c, kc * (c + 1)))
            if c == 0:
                q32 = q.astype(jnp.float32)
                q_norm2 = _dot_t(jnp.ones((BF16_ROWS, q.shape[1]), bf), (q32 * q32).astype(bf))[:1]
                key_head = vt_lo // VT_ROWS
                k_norm = jnp.concatenate([kmax_ref[key_head:key_head + 1, :]] * (tq // LANES), axis=1)
                shift[u] = jnp.broadcast_to(jnp.sqrt(q_norm2) * k_norm, (8, tq))
            s = _dot_t(k, q).reshape(kc // 8, 8, tq)
            p_ref[u % 2, kc * c:kc * (c + 1), :] = jnp.exp2(s - shift[u][None]).reshape(kc, tq).astype(bf)

        for t in range(n_units + 1):
            for c in range(n_chunks):
                if t < n_units:
                    probs(t, c)
                if t >= 1:
                    weighted_values(t - 1, c)
        smallest = jnp.min(functools.reduce(jnp.minimum, min_denom))
        largest = jnp.max(functools.reduce(jnp.maximum, min_denom))
        del min_denom[:]
        return smallest, largest

    def row_max_pipeline():
        col_max = [None] * n_units

        def scores(u, c):
            operands, h, _ = units[u]
            s = _dot_t(*operands(h, slice(kc * c, kc * (c + 1))))
            st_ref[u % 2, kc * c:kc * (c + 1), :] = s
            cm = jnp.max(s.reshape(kc // 8, 8, tq), axis=0)
            col_max[u] = cm if c == 0 else jnp.maximum(col_max[u], cm)
            if c == n_chunks - 1:
                col_max[u] = jnp.broadcast_to(jnp.max(col_max[u], axis=0, keepdims=True), (8, tq))

        def probs(u, c):
            s = st_ref[u % 2, kc * c:kc * (c + 1), :].reshape(kc // 8, 8, tq)
            p = jnp.exp2(s - col_max[u][None]).reshape(kc, tq)
            p_ref[u % 2, kc * c:kc * (c + 1), :] = p.astype(bf)

        for t in range(n_units + PV_LAG):
            for c in range(n_chunks):
                if t < n_units:
                    scores(t, c)
                if 0 <= t - 1 < n_units:
                    probs(t - 1, c)
                if 0 <= t - PV_LAG < n_units:
                    weighted_values(t - PV_LAG, c)
        del min_denom[:]

    smallest, largest = bound_shifted_pipeline()
    in_range = jnp.logical_and(smallest >= MIN_DENOM, largest <= MAX_DENOM)

    def epilogue():
        mla_w = MLA_HEADS * MLA_V
        o = jnp.concatenate([_rms(o_ref[:, :mla_w], gnm_ref[...]),
                             _rms(o_ref[:, mla_w:], gng_ref[...])], axis=-1)
        h_ref[...] = x_ref[...] + _dot(o.astype(bf), wo_ref[...])

    epilogue()

    @pl.when(jnp.logical_not(in_range))
    def _():
        row_max_pipeline()
        epilogue()


def _attention(qm, qg, km, kg, vt, kmax, x2, gnm, gng, wo, batch, seq, tq):
    tokens, d_model = x2.shape
    nq = seq // tq
    row = lambda b, qi: (b * nq + qi, 0)
    per_batch = lambda b, qi: (b, 0)
    const = lambda b, qi: (0, 0)

    def full(a):
        return pl.BlockSpec(a.shape, const)

    return pl.pallas_call(
        _attn_kernel,
        grid=(batch, nq),
        in_specs=[pl.BlockSpec((tq, qm.shape[1]), row), pl.BlockSpec((tq, qg.shape[1]), row),
                  pl.BlockSpec((seq, km.shape[1]), per_batch),
                  pl.BlockSpec((seq, kg.shape[1]), per_batch),
                  pl.BlockSpec((None, vt.shape[1], seq), lambda b, qi: (b, 0, 0)),
                  pl.BlockSpec((None, KN_ROWS, LANES), lambda b, qi: (b, 0, 0)),
                  pl.BlockSpec((tq, d_model), row), full(gnm), full(gng), full(wo)],
        out_specs=pl.BlockSpec((tq, d_model), row),
        out_shape=jax.ShapeDtypeStruct((tokens, d_model), jnp.float32),
        scratch_shapes=[pltpu.VMEM((2, seq, tq), jnp.float32),
                        pltpu.VMEM((2, seq, tq), jnp.bfloat16),
                        pltpu.VMEM((2, VT_ROWS, tq), jnp.float32),
                        pltpu.VMEM((tq, d_model), jnp.float32)],
        compiler_params=pltpu.CompilerParams(
            dimension_semantics=("arbitrary", "arbitrary"), vmem_limit_bytes=VMEM_LIMIT),
        name="attn",
    )(qm, qg, km, kg, vt, kmax, x2, gnm, gng, wo)


def _ffn_kernel(h_ref, hp_ref, hn_ref, g2_ref, wup_ref, cw_ref, cb_ref, wdn_ref, gf_ref,
                o_ref, u_ref, hu_ref, act_ref, *, tiles_per_seq, final_norm):
    tm = h_ref.shape[0]
    bf = jnp.bfloat16
    i = pl.program_id(0)
    si = i % tiles_per_seq
    keep_prev = (si != 0).astype(jnp.float32)
    keep_next = (si != tiles_per_seq - 1).astype(jnp.float32)
    g2 = g2_ref[...]
    u_ref[...] = jnp.concatenate([_rms(hp_ref[...], g2) * keep_prev, _rms(h_ref[...], g2),
                                  _rms(hn_ref[...], g2) * keep_next], axis=0).astype(bf)
    lane_tiles = FF_CHUNK // LANES

    def col(part, c, j=0):
        return part * D_FF + FF_CHUNK * c + LANES * j

    def up(c, part):
        t = _dot(u_ref[...], wup_ref[:, col(part, c):col(part, c) + FF_CHUNK])
        for j in range(lane_tiles):
            hu_ref[c % HU_SLOTS, part, j] = t[:, LANES * j:LANES * (j + 1)]

    def conv(c, part, j):
        lo = col(part, c, j)
        w = cw_ref[:, lo:lo + LANES]
        src = hu_ref.at[c % HU_SLOTS, part, j]
        return (src[pl.ds(HALO - 1, tm), :] * w[0:1] + src[pl.ds(HALO, tm), :] * w[1:2]
                + src[pl.ds(HALO + 1, tm), :] * w[2:3] + cb_ref[:, lo:lo + LANES])

    def act(c, j):
        gate = conv(c, 0, j)
        lo = col(0, c, j)
        act_ref[:, lo:lo + LANES] = (gate * jax.nn.sigmoid(gate) * conv(c, 1, j)).astype(bf)

    assert lane_tiles == 2
    up(0, 0)
    up(0, 1)
    for c in range(N_FF_CHUNKS):
        for half in range(2):
            if c + 1 < N_FF_CHUNKS:
                up(c + 1, half)
            act(c, half)
    out = h_ref[...] + _dot(act_ref[...], wdn_ref[...])
    if final_norm:
        out = _rms(out, gf_ref[...])
    o_ref[...] = out


def _conv_ffn(h2, g2, wup, cw, cb, wdn, gf, seq, tm, final_norm):
    tokens, d_model = h2.shape
    assert (tm + 2 * HALO) % BF16_ROWS == 0
    tiles_per_seq = seq // tm
    n_halo_blocks = tokens // HALO
    per_tile = tm // HALO
    const = lambda i: (0, 0)
    kern = functools.partial(_ffn_kernel, tiles_per_seq=tiles_per_seq, final_norm=final_norm)
    return pl.pallas_call(
        kern,
        grid=(tokens // tm,),
        in_specs=[pl.BlockSpec((tm, d_model), lambda i: (i, 0)),
                  pl.BlockSpec((HALO, d_model), lambda i: (jnp.maximum(i * per_tile - 1, 0), 0)),
                  pl.BlockSpec((HALO, d_model),
                               lambda i: (jnp.minimum((i + 1) * per_tile, n_halo_blocks - 1), 0)),
                  pl.BlockSpec(g2.shape, const),
                  pl.BlockSpec(wup.shape, const, pipeline_mode=pl.Buffered(1)),
                  pl.BlockSpec(cw.shape, const), pl.BlockSpec(cb.shape, const),
                  pl.BlockSpec(wdn.shape, const, pipeline_mode=pl.Buffered(1)),
                  pl.BlockSpec(gf.shape, const)],
        out_specs=pl.BlockSpec((tm, d_model), lambda i: (i, 0)),
        out_shape=jax.ShapeDtypeStruct((tokens, d_model), jnp.float32),
        scratch_shapes=[pltpu.VMEM((tm + 2 * HALO, d_model), jnp.bfloat16),
                        pltpu.VMEM((HU_SLOTS, 2, FF_CHUNK // LANES, tm + 2 * HALO, LANES), jnp.float32),
                        pltpu.VMEM((tm, D_FF), jnp.bfloat16)],
        compiler_params=pltpu.CompilerParams(
            dimension_semantics=("arbitrary",), vmem_limit_bytes=VMEM_LIMIT),
        name="convffn",
    )(h2, h2, h2, g2, wup, cw, cb, wdn, gf)


def _rope_np(pos, dim):
    inv = np.float32(ROPE_THETA) ** (-np.arange(0, dim, 2, dtype=np.float32) / np.float32(dim))
    ang = pos.astype(np.float32)[:, None] * inv[None, :]
    return np.cos(ang), np.sin(ang)


def _lane_tables(cos_a, sin_a, cos_b, sin_b):
    c = np.concatenate([cos_a, cos_b, cos_a, cos_b], axis=-1)
    s = np.concatenate([-sin_a, -sin_b, sin_a, sin_b], axis=-1)
    return jnp.asarray(c), jnp.asarray(s)


def _interleave_pairs(w, width):
    lead = w.shape[:-1]
    w = w.reshape(lead + (-1, 2, 2, width // 2))
    return jnp.swapaxes(w, -3, -2).reshape(lead + (-1,))


def _pick_tile(seq, want):
    t = min(seq, want)
    assert seq % t == 0 and t % HALO == 0
    return t


def kernel(x, norm1_g, w_in, mla_q_norm_g, mla_w_q_up, mla_kv_norm_g, mla_w_kv_up, gqa_q_norm_g,
           gqa_k_norm_g, group_norm_mla_g, group_norm_gqa_g, w_out, norm2_g, ffn_w_up, ffn_conv_w,
           ffn_conv_b, ffn_w_down, final_norm_g):
    batch, seq, d_model = x.shape
    depth = w_in.shape[0]
    bf = jnp.bfloat16
    tm_proj = _pick_tile(seq, PROJ_ROWS)
    tq = _pick_tile(seq, ATTN_QUERIES)
    tm_ffn = _pick_tile(seq, FFN_ROWS)

    t = np.arange(seq)
    cos1, sin1 = _rope_np(t, MLA_ROPE)
    cos_r, sin_r = _rope_np(t // GRID_W, GQA_HEAD_DIM // 2)
    cos_c, sin_c = _rope_np(t % GRID_W, GQA_HEAD_DIM // 2)
    tables = _lane_tables(cos1, sin1, cos1, sin1) + _lane_tables(cos_r, sin_r, cos_c, sin_c)

    sel = np.zeros((MLA_HEADS * MLA_NOPE + LANES, LANES), np.float32)
    rope_lane = np.arange(LANES)
    for h in range(MLA_HEADS):
        sel[MLA_NOPE * h:MLA_NOPE * (h + 1), h] = 1.0
        own = ((rope_lane % (LANES // 2)) < (MLA_ROPE // 2)) == (h % 2 == 0)
        sel[MLA_HEADS * MLA_NOPE + rope_lane[own], h] = 1.0
    sel = jnp.asarray(sel, bf)

    row = lambda v: v.reshape(1, -1)
    h2 = x.reshape(batch * seq, d_model)
    for l in range(depth):
        kpe_w = w_in[l][:, OFF_KPE:OFF_GQ]
        win = jnp.concatenate(
            [w_in[l][:, :OFF_KPE],
             _interleave_pairs(jnp.concatenate([kpe_w, kpe_w], axis=1), MLA_ROPE),
             _interleave_pairs(w_in[l][:, OFF_GQ:OFF_GV], GQA_HEAD_DIM // 2)],
            axis=1).astype(bf)
        winvt = w_in[l][:, OFF_GV:IN_WIDTH].T.astype(bf)
        wq3 = mla_w_q_up[l].reshape(MLA_Q_RANK, MLA_HEADS, MLA_QK)
        wq = jnp.concatenate(
            [wq3[:, :, :MLA_NOPE].reshape(MLA_Q_RANK, -1),
             _interleave_pairs(wq3[:, :, MLA_NOPE:].reshape(MLA_Q_RANK, -1), MLA_ROPE)],
            axis=1).astype(bf)
        wkv3 = mla_w_kv_up[l].reshape(MLA_KV_RANK, MLA_HEADS, MLA_NOPE + MLA_V)
        wkvk = wkv3[:, :, :MLA_NOPE].reshape(MLA_KV_RANK, -1).astype(bf)
        wkvvt = wkv3[:, :, MLA_NOPE:].reshape(MLA_KV_RANK, -1).T.astype(bf)
        qm, km, qg, kg, vt, kn2, wo, wup, wdn = _projections(
            h2, row(norm1_g[l]), win, winvt, row(mla_q_norm_g[l]), wq, row(mla_kv_norm_g[l]),
            wkvk, wkvvt, row(_interleave_pairs(gqa_q_norm_g[l], GQA_HEAD_DIM // 2)),
            row(_interleave_pairs(gqa_k_norm_g[l], GQA_HEAD_DIM // 2)), sel, tables,
            (w_out[l], ffn_w_up[l], ffn_w_down[l]), batch, seq, tm_proj)
        k_mla = jnp.sqrt(jnp.max(kn2[:, :, 0, :MLA_HEADS], axis=0))
        k_gqa = jnp.full((batch, GQA_KV_HEADS),
                         GQA_HEAD_DIM ** 0.5) * jnp.max(jnp.abs(gqa_k_norm_g[l]))
        kmax = jnp.concatenate(
            [k_mla, k_gqa, jnp.zeros((batch, KN_ROWS - N_V_HEADS), jnp.float32)], axis=1)
        kmax = jnp.broadcast_to(kmax[:, :, None], (batch, KN_ROWS, LANES))
        h2 = _attention(qm, qg, km, kg, vt, kmax, h2, row(group_norm_mla_g[l]),
                        row(group_norm_gqa_g[l]), wo, batch, seq, tq)
        h2 = _conv_ffn(h2, row(norm2_g[l]), wup, ffn_conv_w[l], row(ffn_conv_b[l]), wdn,
                       row(final_norm_g), seq, tm_ffn, final_norm=(l == depth - 1))
    return h2.reshape(batch, seq, d_model)
```
